```python
import jax, jax.numpy as jnp
from jax import lax
import numpy as np

D_MODEL = 1024
BATCH = 8
SEQ = 8192
DEPTH = 2

GRID_W = 64
CTX_LEN = 256
N_MIXERS = 2
EXPAND = 2
D_INNER = EXPAND * D_MODEL
CONV_WIDTH = 31
HEAD_DIM = 128
N_Q_HEADS = D_INNER // HEAD_DIM
N_KV_HEADS = N_Q_HEADS // 4
GROUP = N_Q_HEADS // N_KV_HEADS
KV_DIM = N_KV_HEADS * HEAD_DIM
ROPE_AXIS_DIM = HEAD_DIM // 2
ROPE_THETA = 10000.0
Q_BLOCK = 128
RMS_EPS = 1e-6
LN_EPS = 1e-5

kernel_name = "hybrid_conformer_gqa_prefix_dit"


def _rmsnorm(x, g):
    xf = x.astype(jnp.float32)
    y = xf * lax.rsqrt(jnp.mean(xf * xf, axis=-1, keepdims=True) + RMS_EPS)
    return (y * g.astype(jnp.float32)).astype(x.dtype)


def _layernorm(x, g, b):
    xf = x.astype(jnp.float32)
    mu = jnp.mean(xf, axis=-1, keepdims=True)
    xc = xf - mu
    var = jnp.mean(xc * xc, axis=-1, keepdims=True)
    y = xc * lax.rsqrt(var + LN_EPS) * g.astype(jnp.float32) + b.astype(jnp.float32)
    return y.astype(x.dtype)


def _adaln(cond, w, b):
    mod = (jax.nn.silu(cond) @ w + b)[..., None, :]
    return jnp.split(mod, 3, axis=-1)


def _modulate(x, g, shift, scale):
    return _rmsnorm(x, g) * (1.0 + scale) + shift


def _conformer_conv_mixer(h, w_in, b_in, dw_w, dw_b, ln_g, ln_b, w_out, b_out):
    u = h @ w_in + b_in
    a, g_lin, z = jnp.split(u, 3, axis=-1)
    v = a * jax.nn.sigmoid(g_lin)
    v = lax.conv_general_dilated(
        v, dw_w[:, None, :].astype(v.dtype), window_strides=(1,),
        padding=[(CONV_WIDTH // 2, CONV_WIDTH // 2)],
        dimension_numbers=('NWC', 'WIO', 'NWC'),
        feature_group_count=D_INNER) + dw_b
    v = jax.nn.silu(_layernorm(v, ln_g, ln_b))
    return (v * jax.nn.silu(z)) @ w_out + b_out


def _axial_rope_tables(length):
    rows = length // GRID_W
    row = jnp.broadcast_to(jnp.arange(rows)[:, None], (rows, GRID_W)).reshape(-1)
    col = jnp.broadcast_to(jnp.arange(GRID_W)[None, :], (rows, GRID_W)).reshape(-1)
    pos = jnp.stack([row, col], axis=-1).astype(jnp.float32)
    inv_freq = ROPE_THETA ** (-jnp.arange(0, ROPE_AXIS_DIM, 2, dtype=jnp.float32) / ROPE_AXIS_DIM)
    ang = pos[:, :, None] * inv_freq
    return jnp.cos(ang), jnp.sin(ang)


def _apply_axial_rope(x, cos, sin):
    B, L, H, _ = x.shape
    xr = x.astype(jnp.float32).reshape(B, L, H, 2, 2, ROPE_AXIS_DIM // 2)
    x1, x2 = xr[..., 0, :], xr[..., 1, :]
    c, s = cos[:, None], sin[:, None]
    out = jnp.stack([x1 * c - x2 * s, x1 * s + x2 * c], axis=-2)
    return out.reshape(B, L, H, HEAD_DIM).astype(x.dtype)


def _split_heads(t, n_heads):
    return t.reshape(t.shape[0], t.shape[1], n_heads, HEAD_DIM)


def _gqa_attend(q, k, v):
    B, Lq = q.shape[0], q.shape[1]
    nb = Lq // Q_BLOCK
    qb = (q * (HEAD_DIM ** -0.5)).reshape(B, nb, Q_BLOCK, N_KV_HEADS, GROUP, HEAD_DIM)
    qb = qb.transpose(1, 0, 2, 3, 4, 5)

    def one_block(q_blk):
        s = jnp.einsum('bqhgd,bkhd->bhgqk', q_blk, k).astype(jnp.float32)
        p = jax.nn.softmax(s, axis=-1).astype(v.dtype)
        return jnp.einsum('bhgqk,bkhd->bqhgd', p, v)

    o = lax.map(one_block, qb)
    return o.transpose(1, 0, 2, 3, 4, 5).reshape(B, Lq, N_Q_HEADS * HEAD_DIM)


def _attn_mixer(h_lat, h_ctx, w_in, q_norm_g, k_norm_g, w_out, ctx_out):
    q_end, k_end, v_end = D_INNER, D_INNER + KV_DIM, D_INNER + 2 * KV_DIM
    kv_ctx = h_ctx @ w_in[:, q_end:v_end]
    k_ctx = _rmsnorm(_split_heads(kv_ctx[..., :KV_DIM], N_KV_HEADS), k_norm_g)
    v_ctx = _split_heads(kv_ctx[..., KV_DIM:], N_KV_HEADS)
    u = h_lat @ w_in
    q, k, v, z = jnp.split(u, [q_end, k_end, v_end], axis=-1)
    cos, sin = _axial_rope_tables(h_lat.shape[1])
    q = _apply_axial_rope(_rmsnorm(_split_heads(q, N_Q_HEADS), q_norm_g), cos, sin)
    k = _apply_axial_rope(_rmsnorm(_split_heads(k, N_KV_HEADS), k_norm_g), cos, sin)
    v = _split_heads(v, N_KV_HEADS)
    k_all = jnp.concatenate([k_ctx, k], axis=1)
    v_all = jnp.concatenate([v_ctx, v], axis=1)
    y_lat = (_gqa_attend(q, k_all, v_all) * jax.nn.silu(z)) @ w_out
    if not ctx_out:
        return y_lat, None
    q_ctx = _rmsnorm(_split_heads(h_ctx @ w_in[:, :q_end], N_Q_HEADS), q_norm_g)
    z_ctx = h_ctx @ w_in[:, v_end:]
    y_ctx = (_gqa_attend(q_ctx, k_ctx, v_ctx) * jax.nn.silu(z_ctx)) @ w_out
    return y_lat, y_ctx


def setup_inputs(seed: int = 0) -> dict:
    key = jax.random.key(seed)
    ks = jax.random.split(key, 32)
    f32 = jnp.float32
    D, E = D_MODEL, D_INNER

    def nrm(k, shape, scale):
        return jax.random.normal(k, shape, f32) * scale

    return {
        "x": nrm(ks[0], (BATCH, SEQ, D), 1.0),
        "c": nrm(ks[1], (BATCH, D), 1.0),
        "ctx": nrm(ks[2], (BATCH, CTX_LEN, D), 1.0),
        "c_ctx": nrm(ks[3], (D,), 1.0),
        "l0_norm_g": 1.0 + nrm(ks[4], (D,), 0.02),
        "l0_ada_w": nrm(ks[5], (D, 3 * D), D ** -0.5),
        "l0_ada_b": nrm(ks[6], (3 * D,), 0.01),
        "l0_w_in": nrm(ks[7], (D, 3 * E), D ** -0.5),
        "l0_b_in": nrm(ks[8], (3 * E,), 0.01),
        "l0_dw_w": nrm(ks[9], (CONV_WIDTH, E), CONV_WIDTH ** -0.5),
        "l0_dw_b": nrm(ks[10], (E,), 0.01),
        "l0_ln_g": 1.0 + nrm(ks[11], (E,), 0.02),
        "l0_ln_b": nrm(ks[12], (E,), 0.01),
        "l0_w_out": nrm(ks[13], (E, D), E ** -0.5),
        "l0_b_out": nrm(ks[14], (D,), 0.01),
        "l1_norm_g": 1.0 + nrm(ks[15], (D,), 0.02),
        "l1_ada_w": nrm(ks[16], (D, 3 * D), D ** -0.5),
        "l1_ada_b": nrm(ks[17], (3 * D,), 0.01),
        "l1_w_in": nrm(ks[18], (D, 2 * E + 2 * KV_DIM), D ** -0.5),
        "l1_q_norm_g": 1.0 + nrm(ks[19], (HEAD_DIM,), 0.02),
        "l1_k_norm_g": 1.0 + nrm(ks[20], (HEAD_DIM,), 0.02),
        "l1_w_out": nrm(ks[21], (E, D), E ** -0.5),
        "final_norm_g": 1.0 + nrm(ks[22], (D,), 0.02),
    }


def reference(x, c, ctx, c_ctx,
              l0_norm_g, l0_ada_w, l0_ada_b, l0_w_in, l0_b_in, l0_dw_w, l0_dw_b,
              l0_ln_g, l0_ln_b, l0_w_out, l0_b_out,
              l1_norm_g, l1_ada_w, l1_ada_b, l1_w_in, l1_q_norm_g, l1_k_norm_g, l1_w_out,
              final_norm_g):
    layers = [
        (l0_norm_g, l0_ada_w, l0_ada_b,
         (l0_w_in, l0_b_in, l0_dw_w, l0_dw_b, l0_ln_g, l0_ln_b, l0_w_out, l0_b_out)),
        (l1_norm_g, l1_ada_w, l1_ada_b,
         (l1_w_in, l1_q_norm_g, l1_k_norm_g, l1_w_out)),
    ]
    for i in range(DEPTH):
        norm_g, ada_w, ada_b, mix = layers[i]
        last = i == DEPTH - 1
        shift, scale, gate = _adaln(c, ada_w, ada_b)
        shift_c, scale_c, gate_c = _adaln(c_ctx, ada_w, ada_b)
        h = _modulate(x, norm_g, shift, scale)
        h_c = _modulate(ctx, norm_g, shift_c, scale_c)
        if i % N_MIXERS == 0:
            y = _conformer_conv_mixer(h, *mix)
            y_c = None if last else _conformer_conv_mixer(h_c, *mix)
        else:
            y, y_c = _attn_mixer(h, h_c, *mix, ctx_out=not last)
        x = x + gate * y
        if not last:
            ctx = ctx + gate_c * y_c
    return _rmsnorm(x, final_norm_g)
```

```python
import functools
import math

import jax
import jax.numpy as jnp
from jax import lax
from jax.experimental import pallas as pl
from jax.experimental.pallas import tpu as pltpu

D_MODEL = 1024
GRID_W = 64
D_INNER = 2048
CONV_WIDTH = 31
HEAD_DIM = 128
N_Q_HEADS = 16
N_KV_HEADS = 4
GROUP = 4
KV_DIM = 512
ROPE_AXIS_DIM = 64
ROPE_THETA = 10000.0
RMS_EPS = 1e-6
LN_EPS = 1e-5

HALO = 16
COND_ROWS = 16
VMEM_LIMIT = 56 * 1024 * 1024

_BF16 = jnp.bfloat16
_F32 = jnp.float32


def _sigmoid(x):
    return 1.0 / (1.0 + jnp.exp(-x))


def _silu(x):
    return x * _sigmoid(x)


def _dot(a, b):
    return jnp.dot(a, b, preferred_element_type=_F32)


def _const_spec(shape):
    nd = len(shape)
    return pl.BlockSpec(shape, lambda *_: (0,) * nd, pipeline_mode=pl.Buffered(1))


def _ada_kernel(cond_ref, w0_ref, b0_ref, w1_ref, b1_ref, o0_ref, o1_ref):
    s = _silu(cond_ref[...])
    o0_ref[...] = jnp.dot(s, w0_ref[...], preferred_element_type=_F32,
                          precision=lax.Precision.HIGHEST) + b0_ref[...]
    o1_ref[...] = jnp.dot(s, w1_ref[...], preferred_element_type=_F32,
                          precision=lax.Precision.HIGHEST) + b1_ref[...]


def _ada(cond, w0, b0, w1, b1):
    d = D_MODEL
    wspec = pl.BlockSpec((d, d), lambda n: (0, n))
    bspec = pl.BlockSpec((1, d), lambda n: (0, n))
    ospec = pl.BlockSpec((COND_ROWS, d), lambda n: (0, n))
    return pl.pallas_call(
        _ada_kernel,
        grid=(3,),
        in_specs=[pl.BlockSpec((COND_ROWS, d), lambda n: (0, 0)), wspec, bspec, wspec, bspec],
        out_specs=[ospec, ospec],
        out_shape=[jax.ShapeDtypeStruct((COND_ROWS, 3 * d), _F32)] * 2,
        compiler_params=pltpu.CompilerParams(vmem_limit_bytes=VMEM_LIMIT),
        name="adaln",
    )(cond, w0, b0.reshape(1, -1), w1, b1.reshape(1, -1))


def _modulate(xf, gmul, shift):
    ms = jnp.mean(xf * xf, axis=-1, keepdims=True)
    return xf * lax.rsqrt(ms + RMS_EPS) * gmul + shift


def _l0_kernel(xm_ref, xp_ref, xn_ref, mod_ref, ng_ref, win_ref, bin_ref, dww_ref, dwb_ref,
               lng_ref, lnb_ref, wout_ref, bout_ref, o_ref,
               h_ref, v_ref, conv_ref, zs_ref, *, tile, seq_len, cblk, rblk):
    d, e = D_MODEL, D_INNER
    j = pl.program_id(1)
    shift = mod_ref[0, :, 0:d]
    scale = mod_ref[0, :, d:2 * d]
    gate = mod_ref[0, :, 2 * d:3 * d]
    gmul = ng_ref[...] * (1.0 + scale)

    xm = xm_ref[0]
    h_ref[0:HALO, :] = _modulate(xp_ref[0], gmul, shift).astype(_BF16)
    h_ref[HALO:HALO + tile, :] = _modulate(xm, gmul, shift).astype(_BF16)
    h_ref[HALO + tile:, :] = _modulate(xn_ref[0], gmul, shift).astype(_BF16)

    tok = j * tile - HALO + lax.broadcasted_iota(jnp.int32, (tile + 2 * HALO, 1), 0)
    valid = (tok >= 0) & (tok < seq_len)

    def body(cb, carry):
        off = pl.multiple_of(cb * cblk, cblk)
        hh = h_ref[...]
        a = _dot(hh, win_ref[:, pl.ds(off, cblk)]) + bin_ref[:, pl.ds(off, cblk)]
        g = _dot(hh, win_ref[:, pl.ds(e + off, cblk)]) + bin_ref[:, pl.ds(e + off, cblk)]
        v_ref[...] = jnp.where(valid, a * _sigmoid(g), 0.0)
        z = _dot(h_ref[HALO:HALO + tile, :], win_ref[:, pl.ds(2 * e + off, cblk)])
        z = z + bin_ref[:, pl.ds(2 * e + off, cblk)]
        zs_ref[:, pl.ds(off, cblk)] = _silu(z)
        for rb in range(tile // rblk):
            acc = jnp.broadcast_to(dwb_ref[:, pl.ds(off, cblk)], (rblk, cblk))
            for k in range(CONV_WIDTH):
                r0 = rb * rblk + (HALO - CONV_WIDTH // 2) + k
                acc = acc + v_ref[r0:r0 + rblk, :] * dww_ref[k:k + 1, pl.ds(off, cblk)]
            conv_ref[rb * rblk:(rb + 1) * rblk, pl.ds(off, cblk)] = acc
        return carry

    lax.fori_loop(0, e // cblk, body, 0)

    cv = conv_ref[...]
    mu = jnp.mean(cv, axis=-1, keepdims=True)
    xc = cv - mu
    var = jnp.mean(xc * xc, axis=-1, keepdims=True)
    y = _silu(xc * lax.rsqrt(var + LN_EPS) * lng_ref[...] + lnb_ref[...])
    gated = (y * zs_ref[...]).astype(_BF16)
    out = _dot(gated, wout_ref[...]) + bout_ref[...]
    o_ref[0] = xm + gate * out


def _layer0(x, mod, mod_row, ng, win, b_in, dww, dwb, lng, lnb, wout, bout, *, tile):
    bsz, seq_len, d = x.shape
    e = D_INNER
    nt = seq_len // tile
    hb = tile // HALO
    n_hblk = seq_len // HALO
    cblk, rblk = 256, 64
    if mod_row is None:
        mod_idx = lambda b, j: (b, 0, 0)
    else:
        mod_idx = lambda b, j: (mod_row, 0, 0)
    kern = functools.partial(_l0_kernel, tile=tile, seq_len=seq_len, cblk=cblk, rblk=rblk)
    return pl.pallas_call(
        kern,
        grid=(bsz, nt),
        in_specs=[
            pl.BlockSpec((1, tile, d), lambda b, j: (b, j, 0)),
            pl.BlockSpec((1, HALO, d), lambda b, j: (b, jnp.maximum(j * hb - 1, 0), 0)),
            pl.BlockSpec((1, HALO, d), lambda b, j: (b, jnp.minimum((j + 1) * hb, n_hblk - 1), 0)),
            pl.BlockSpec((1, 1, 3 * d), mod_idx),
            _const_spec((1, d)),
            _const_spec((d, 3 * e)),
            _const_spec((1, 3 * e)),
            _const_spec((CONV_WIDTH, e)),
            _const_spec((1, e)),
            _const_spec((1, e)),
            _const_spec((1, e)),
            _const_spec((e, d)),
            _const_spec((1, d)),
        ],
        out_specs=pl.BlockSpec((1, tile, d), lambda b, j: (b, j, 0)),
        out_shape=jax.ShapeDtypeStruct((bsz, seq_len, d), _F32),
        scratch_shapes=[
            pltpu.VMEM((tile + 2 * HALO, d), _BF16),
            pltpu.VMEM((tile + 2 * HALO, cblk), _F32),
            pltpu.VMEM((tile, e), _F32),
            pltpu.VMEM((tile, e), _F32),
        ],
        compiler_params=pltpu.CompilerParams(
            dimension_semantics=("parallel", "parallel"), vmem_limit_bytes=VMEM_LIMIT),
        name="layer0_conv",
    )(x, x, x, mod, ng, win, b_in, dww, dwb, lng, lnb, wout, bout)


def _head_rmsnorm(xh, g):
    ms = jnp.mean(xh * xh, axis=-1, keepdims=True)
    return xh * lax.rsqrt(ms + RMS_EPS) * g


def _rope(xh, cos, sin_signed):
    lane = lax.broadcasted_iota(jnp.int32, xh.shape, 1)
    first = (lane % ROPE_AXIS_DIM) < (ROPE_AXIS_DIM // 2)
    partner = jnp.where(first, pltpu.roll(xh, HEAD_DIM - ROPE_AXIS_DIM // 2, 1),
                        pltpu.roll(xh, ROPE_AXIS_DIM // 2, 1))
    return xh * cos + partner * sin_signed


def _l1_in_kernel(*refs, tile, latent):
    d, e, hd = D_MODEL, D_INNER, HEAD_DIM
    if latent:
        (x_ref, mod_ref, ng_ref, w_ref, qg_ref, kg_ref, cos_ref, sin_ref,
         q_ref, k_ref, v_ref, zs_ref) = refs
    else:
        x_ref, mod_ref, ng_ref, w_ref, kg_ref, k_ref, v_ref = refs
    shift = mod_ref[0, :, 0:d]
    scale = mod_ref[0, :, d:2 * d]
    gmul = ng_ref[...] * (1.0 + scale)
    h = _modulate(x_ref[0], gmul, shift).astype(_BF16)

    if latent:
        cos = cos_ref[...]
        sin = sin_ref[...]
        qg = qg_ref[...]
        for c in range(e // KV_DIM):
            u = _dot(h, w_ref[:, c * KV_DIM:(c + 1) * KV_DIM])
            for i in range(KV_DIM // hd):
                qh = _rope(_head_rmsnorm(u[:, i * hd:(i + 1) * hd], qg), cos, sin)
                q_ref[0, :, c * KV_DIM + i * hd:c * KV_DIM + (i + 1) * hd] = qh.astype(_BF16)
        for c in range(e // KV_DIM):
            z = _dot(h, w_ref[:, e + 2 * KV_DIM + c * KV_DIM:e + 2 * KV_DIM + (c + 1) * KV_DIM])
            zs_ref[0, :, c * KV_DIM:(c + 1) * KV_DIM] = _silu(z).astype(_BF16)

    kg = kg_ref[...]
    ku = _dot(h, w_ref[:, e:e + KV_DIM])
    for i in range(N_KV_HEADS):
        kh = _head_rmsnorm(ku[:, i * hd:(i + 1) * hd], kg)
        if latent:
            kh = _rope(kh, cos, sin)
        k_ref[0, :, i * hd:(i + 1) * hd] = kh.astype(_BF16)
    v_ref[0] = _dot(h, w_ref[:, e + KV_DIM:e + 2 * KV_DIM]).astype(_BF16)


def _layer1_in(x, mod, mod_row, ng, w, qg, kg, cos, sin, *, tile, latent):
    bsz, seq_len, d = x.shape
    e = D_INNER
    nt = seq_len // tile
    ncol = 2 * e + 2 * KV_DIM
    if mod_row is None:
        mod_idx = lambda b, j: (b, 0, 0)
    else:
        mod_idx = lambda b, j: (mod_row, 0, 0)
    tok = lambda width: pl.BlockSpec((1, tile, width), lambda b, j: (b, j, 0))
    in_specs = [tok(d), pl.BlockSpec((1, 1, 3 * d), mod_idx), _const_spec((1, d)),
                _const_spec((d, ncol))]
    args = [x, mod, ng, w]
    kv_shape = jax.ShapeDtypeStruct((bsz, seq_len, KV_DIM), _BF16)
    if latent:
        pos = pl.BlockSpec((tile, HEAD_DIM), lambda b, j: (j, 0))
        in_specs += [_const_spec((1, HEAD_DIM)), _const_spec((1, HEAD_DIM)), pos, pos]
        args += [qg, kg, cos, sin]
        wide = jax.ShapeDtypeStruct((bsz, seq_len, e), _BF16)
        out_specs = [tok(e), tok(KV_DIM), tok(KV_DIM), tok(e)]
        out_shape = [wide, kv_shape, kv_shape, wide]
    else:
        in_specs += [_const_spec((1, HEAD_DIM))]
        args += [kg]
        out_specs = [tok(KV_DIM), tok(KV_DIM)]
        out_shape = [kv_shape, kv_shape]
    return pl.pallas_call(
        functools.partial(_l1_in_kernel, tile=tile, latent=latent),
        grid=(bsz, nt),
        in_specs=in_specs,
        out_specs=out_specs,
        out_shape=out_shape,
        compiler_params=pltpu.CompilerParams(
            dimension_semantics=("parallel", "parallel"), vmem_limit_bytes=VMEM_LIMIT),
        name="layer1_in_latent" if latent else "layer1_in_ctx",
    )(*args)


def _attn_kernel(q_ref, kc_ref, vc_ref, k_ref, v_ref, zs_ref, y_ref, *, tq, tk, n_chunks):
    hd = HEAD_DIM
    q4 = q_ref[0]
    qs = jnp.concatenate([q4[:, i * hd:(i + 1) * hd] for i in range(GROUP)], axis=0)
    rows = GROUP * tq

    def step(kc, vc, m, l, acc):
        s = lax.dot_general(qs, kc, (((1,), (1,)), ((), ())), preferred_element_type=_F32)
        m_new = jnp.maximum(m, jnp.max(s, axis=-1, keepdims=True))
        p = jnp.exp2(s - m_new)
        alpha = jnp.exp2(m - m_new)
        l_new = alpha * l + jnp.sum(p, axis=-1, keepdims=True)
        acc_new = alpha * acc + _dot(p.astype(_BF16), vc)
        return m_new, l_new, acc_new

    m0 = jnp.full((rows, 1), -jnp.inf, _F32)
    l0 = jnp.zeros((rows, 1), _F32)
    a0 = jnp.zeros((rows, hd), _F32)
    m, l, acc = step(kc_ref[0], vc_ref[0], m0, l0, a0)

    def body(c, carry):
        off = pl.multiple_of(c * tk, tk)
        return step(k_ref[0, pl.ds(off, tk), :], v_ref[0, pl.ds(off, tk), :], *carry)

    m, l, acc = lax.fori_loop(0, n_chunks, body, (m, l, acc))
    o = acc / l
    for i in range(GROUP):
        zi = zs_ref[0, :, i * hd:(i + 1) * hd].astype(_F32)
        y_ref[0, :, i * hd:(i + 1) * hd] = (o[i * tq:(i + 1) * tq, :] * zi).astype(_BF16)


def _attention(q, k_ctx, v_ctx, k, v, zs, *, tq, tk):
    bsz, seq_len, e = q.shape
    ctx_len = k_ctx.shape[1]
    hd = HEAD_DIM
    qspec = pl.BlockSpec((1, tq, KV_DIM), lambda b, g, i: (b, i, g))
    kern = functools.partial(_attn_kernel, tq=tq, tk=tk, n_chunks=seq_len // tk)
    return pl.pallas_call(
        kern,
        grid=(bsz, N_KV_HEADS, seq_len // tq),
        in_specs=[
            qspec,
            pl.BlockSpec((1, ctx_len, hd), lambda b, g, i: (b, 0, g)),
            pl.BlockSpec((1, ctx_len, hd), lambda b, g, i: (b, 0, g)),
            pl.BlockSpec((1, seq_len, hd), lambda b, g, i: (b, 0, g)),
            pl.BlockSpec((1, seq_len, hd), lambda b, g, i: (b, 0, g)),
            qspec,
        ],
        out_specs=qspec,
        out_shape=jax.ShapeDtypeStruct((bsz, seq_len, e), _BF16),
        compiler_params=pltpu.CompilerParams(
            dimension_semantics=("parallel", "parallel", "parallel"), vmem_limit_bytes=VMEM_LIMIT),
        name="gqa_attention",
    )(q, k_ctx, v_ctx, k, v, zs)


def _out_kernel(y_ref, x_ref, mod_ref, w_ref, fg_ref, o_ref):
    d = D_MODEL
    gate = mod_ref[0, :, 2 * d:3 * d]
    xo = x_ref[0] + gate * _dot(y_ref[0], w_ref[...])
    ms = jnp.mean(xo * xo, axis=-1, keepdims=True)
    o_ref[0] = xo * lax.rsqrt(ms + RMS_EPS) * fg_ref[...]


def _out_proj(y, x, mod, w, fg, *, tile):
    bsz, seq_len, d = x.shape
    e = D_INNER
    return pl.pallas_call(
        _out_kernel,
        grid=(bsz, seq_len // tile),
        in_specs=[
            pl.BlockSpec((1, tile, e), lambda b, j: (b, j, 0)),
            pl.BlockSpec((1, tile, d), lambda b, j: (b, j, 0)),
            pl.BlockSpec((1, 1, 3 * d), lambda b, j: (b, 0, 0)),
            _const_spec((e, d)),
            _const_spec((1, d)),
        ],
        out_specs=pl.BlockSpec((1, tile, d), lambda b, j: (b, j, 0)),
        out_shape=jax.ShapeDtypeStruct((bsz, seq_len, d), _F32),
        compiler_params=pltpu.CompilerParams(
            dimension_semantics=("parallel", "parallel"), vmem_limit_bytes=VMEM_LIMIT),
        name="out_proj_norm",
    )(y, x, mod, w, fg)


def _rope_tables(length):
    rows = length // GRID_W
    row = jnp.broadcast_to(jnp.arange(rows)[:, None], (rows, GRID_W)).reshape(-1)
    col = jnp.broadcast_to(jnp.arange(GRID_W)[None, :], (rows, GRID_W)).reshape(-1)
    inv_freq = ROPE_THETA ** (-jnp.arange(0, ROPE_AXIS_DIM, 2, dtype=_F32) / ROPE_AXIS_DIM)
    ang_r = row.astype(_F32)[:, None] * inv_freq
    ang_c = col.astype(_F32)[:, None] * inv_freq
    cos = jnp.concatenate([jnp.cos(ang_r)] * 2 + [jnp.cos(ang_c)] * 2, axis=-1)
    sin = jnp.concatenate([-jnp.sin(ang_r), jnp.sin(ang_r), -jnp.sin(ang_c), jnp.sin(ang_c)], axis=-1)
    return cos, sin


def kernel(x, c, ctx, c_ctx, l0_norm_g, l0_ada_w, l0_ada_b, l0_w_in, l0_b_in, l0_dw_w, l0_dw_b, l0_ln_g, l0_ln_b, l0_w_out, l0_b_out, l1_norm_g, l1_ada_w, l1_ada_b, l1_w_in, l1_q_norm_g, l1_k_norm_g, l1_w_out, final_norm_g):
    bsz, seq_len, d = x.shape
    ctx_len = ctx.shape[1]
    row = lambda a: a.reshape(1, -1)

    cond = jnp.zeros((COND_ROWS, d), _F32).at[:bsz].set(c).at[bsz].set(c_ctx)
    mod0, mod1 = _ada(cond, l0_ada_w, l0_ada_b, l1_ada_w, l1_ada_b)
    mod0 = mod0.reshape(COND_ROWS, 1, 3 * d)
    mod1 = mod1.reshape(COND_ROWS, 1, 3 * d)

    l0_args = (row(l0_norm_g), l0_w_in.astype(_BF16), row(l0_b_in), l0_dw_w, row(l0_dw_b),
               row(l0_ln_g), row(l0_ln_b), l0_w_out.astype(_BF16), row(l0_b_out))
    x1 = _layer0(x, mod0, None, *l0_args, tile=512)
    ctx1 = _layer0(ctx, mod0, bsz, *l0_args, tile=ctx_len)

    qg = row(l1_q_norm_g) * (HEAD_DIM ** -0.5 * math.log2(math.e))
    kg = row(l1_k_norm_g)
    w1 = l1_w_in.astype(_BF16)
    cos, sin = _rope_tables(seq_len)
    q, k, v, zs = _layer1_in(x1, mod1, None, row(l1_norm_g), w1, qg, kg, cos, sin,
                             tile=512, latent=True)
    k_ctx, v_ctx = _layer1_in(ctx1, mod1, bsz, row(l1_norm_g), w1, None, kg, None, None,
                              tile=ctx_len, latent=False)

    y = _attention(q, k_ctx, v_ctx, k, v, zs, tq=128, tk=512)
    return _out_proj(y, x1, mod1, l1_w_out.astype(_BF16), row(final_norm_g), tile=512)
```

```python
import functools
import math

import jax
import jax.numpy as jnp
from jax import lax
from jax.experimental import pallas as pl
from jax.experimental.pallas import tpu as pltpu

D_MODEL = 1024
GRID_W = 64
D_INNER = 2048
CONV_WIDTH = 31
HEAD_DIM = 128
N_Q_HEADS = 16
N_KV_HEADS = 4
GROUP = 4
KV_DIM = 512
ROPE_AXIS_DIM = 64
ROPE_THETA = 10000.0
RMS_EPS = 1e-6
LN_EPS = 1e-5

HALO = 16
COND_ROWS = 16
VMEM_LIMIT = 56 * 1024 * 1024
MAX_UNSHIFTED_SCORE = 64.0
SCORE_BOUND_MARGIN = 1.02

_BF16 = jnp.bfloat16
_F32 = jnp.float32


def _sigmoid(x):
    return 1.0 / (1.0 + jnp.exp(-x))


def _silu(x):
    return x * _sigmoid(x)


def _dot(a, b):
    return jnp.dot(a, b, preferred_element_type=_F32)


def _const_spec(shape):
    nd = len(shape)
    return pl.BlockSpec(shape, lambda *_: (0,) * nd, pipeline_mode=pl.Buffered(1))


def _ada_kernel(cond_ref, w0_ref, b0_ref, w1_ref, b1_ref, o0_ref, o1_ref):
    s = _silu(cond_ref[...])
    o0_ref[...] = jnp.dot(s, w0_ref[...], preferred_element_type=_F32,
                          precision=lax.Precision.HIGHEST) + b0_ref[...]
    o1_ref[...] = jnp.dot(s, w1_ref[...], preferred_element_type=_F32,
                          precision=lax.Precision.HIGHEST) + b1_ref[...]


def _ada(cond, w0, b0, w1, b1):
    d = D_MODEL
    wspec = pl.BlockSpec((d, d), lambda n: (0, n))
    bspec = pl.BlockSpec((1, d), lambda n: (0, n))
    ospec = pl.BlockSpec((COND_ROWS, d), lambda n: (0, n))
    return pl.pallas_call(
        _ada_kernel,
        grid=(3,),
        in_specs=[pl.BlockSpec((COND_ROWS, d), lambda n: (0, 0)), wspec, bspec, wspec, bspec],
        out_specs=[ospec, ospec],
        out_shape=[jax.ShapeDtypeStruct((COND_ROWS, 3 * d), _F32)] * 2,
        compiler_params=pltpu.CompilerParams(vmem_limit_bytes=VMEM_LIMIT),
        name="adaln",
    )(cond, w0, b0.reshape(1, -1), w1, b1.reshape(1, -1))


def _modulate(xf, gmul, shift):
    ms = jnp.mean(xf * xf, axis=-1, keepdims=True)
    return xf * lax.rsqrt(ms + RMS_EPS) * gmul + shift


def _l0_kernel(xm_ref, xp_ref, xn_ref, mod_ref, ng_ref, win_ref, bin_ref, dww_ref, dwb_ref,
               lng_ref, lnb_ref, wout_ref, bout_ref, o_ref,
               h_ref, v_ref, conv_ref, zs_ref, *, tile, seq_len, cblk, rblk):
    d, e = D_MODEL, D_INNER
    j = pl.program_id(1)
    shift = mod_ref[0, :, 0:d]
    scale = mod_ref[0, :, d:2 * d]
    gate = mod_ref[0, :, 2 * d:3 * d]
    gmul = ng_ref[...] * (1.0 + scale)

    xm = xm_ref[0]
    h_ref[0:HALO, :] = _modulate(xp_ref[0], gmul, shift).astype(_BF16)
    h_ref[HALO:HALO + tile, :] = _modulate(xm, gmul, shift).astype(_BF16)
    h_ref[HALO + tile:, :] = _modulate(xn_ref[0], gmul, shift).astype(_BF16)

    tok = j * tile - HALO + lax.broadcasted_iota(jnp.int32, (tile + 2 * HALO, 1), 0)
    valid = (tok >= 0) & (tok < seq_len)

    def body(cb, carry):
        off = pl.multiple_of(cb * cblk, cblk)
        hh = h_ref[...]
        a = _dot(hh, win_ref[:, pl.ds(off, cblk)]) + bin_ref[:, pl.ds(off, cblk)]
        g = _dot(hh, win_ref[:, pl.ds(e + off, cblk)]) + bin_ref[:, pl.ds(e + off, cblk)]
        v_ref[...] = jnp.where(valid, a * _sigmoid(g), 0.0)
        z = _dot(h_ref[HALO:HALO + tile, :], win_ref[:, pl.ds(2 * e + off, cblk)])
        z = z + bin_ref[:, pl.ds(2 * e + off, cblk)]
        zs_ref[:, pl.ds(off, cblk)] = _silu(z)
        for rb in range(tile // rblk):
            acc = jnp.broadcast_to(dwb_ref[:, pl.ds(off, cblk)], (rblk, cblk))
            for k in range(CONV_WIDTH):
                r0 = rb * rblk + (HALO - CONV_WIDTH // 2) + k
                acc = acc + v_ref[r0:r0 + rblk, :] * dww_ref[k:k + 1, pl.ds(off, cblk)]
            conv_ref[rb * rblk:(rb + 1) * rblk, pl.ds(off, cblk)] = acc
        return carry

    lax.fori_loop(0, e // cblk, body, 0)

    cv = conv_ref[...]
    mu = jnp.mean(cv, axis=-1, keepdims=True)
    xc = cv - mu
    var = jnp.mean(xc * xc, axis=-1, keepdims=True)
    y = _silu(xc * lax.rsqrt(var + LN_EPS) * lng_ref[...] + lnb_ref[...])
    gated = (y * zs_ref[...]).astype(_BF16)
    out = _dot(gated, wout_ref[...]) + bout_ref[...]
    o_ref[0] = xm + gate * out


def _layer0(x, mod, mod_row, ng, win, b_in, dww, dwb, lng, lnb, wout, bout, *, tile):
    bsz, seq_len, d = x.shape
    e = D_INNER
    nt = seq_len // tile
    hb = tile // HALO
    n_hblk = seq_len // HALO
    cblk, rblk = 256, 64
    if mod_row is None:
        mod_idx = lambda b, j: (b, 0, 0)
    else:
        mod_idx = lambda b, j: (mod_row, 0, 0)
    kern = functools.partial(_l0_kernel, tile=tile, seq_len=seq_len, cblk=cblk, rblk=rblk)
    return pl.pallas_call(
        kern,
        grid=(bsz, nt),
        in_specs=[
            pl.BlockSpec((1, tile, d), lambda b, j: (b, j, 0)),
            pl.BlockSpec((1, HALO, d), lambda b, j: (b, jnp.maximum(j * hb - 1, 0), 0)),
            pl.BlockSpec((1, HALO, d), lambda b, j: (b, jnp.minimum((j + 1) * hb, n_hblk - 1), 0)),
            pl.BlockSpec((1, 1, 3 * d), mod_idx),
            _const_spec((1, d)),
            _const_spec((d, 3 * e)),
            _const_spec((1, 3 * e)),
            _const_spec((CONV_WIDTH, e)),
            _const_spec((1, e)),
            _const_spec((1, e)),
            _const_spec((1, e)),
            _const_spec((e, d)),
            _const_spec((1, d)),
        ],
        out_specs=pl.BlockSpec((1, tile, d), lambda b, j: (b, j, 0)),
        out_shape=jax.ShapeDtypeStruct((bsz, seq_len, d), _F32),
        scratch_shapes=[
            pltpu.VMEM((tile + 2 * HALO, d), _BF16),
            pltpu.VMEM((tile + 2 * HALO, cblk), _F32),
            pltpu.VMEM((tile, e), _F32),
            pltpu.VMEM((tile, e), _F32),
        ],
        compiler_params=pltpu.CompilerParams(
            dimension_semantics=("parallel", "parallel"), vmem_limit_bytes=VMEM_LIMIT),
        name="layer0_conv",
    )(x, x, x, mod, ng, win, b_in, dww, dwb, lng, lnb, wout, bout)


def _head_rmsnorm(xh, g):
    ms = jnp.mean(xh * xh, axis=-1, keepdims=True)
    return xh * lax.rsqrt(ms + RMS_EPS) * g


def _rope(xh, cos, sin_signed):
    lane = lax.broadcasted_iota(jnp.int32, xh.shape, 1)
    first = (lane % ROPE_AXIS_DIM) < (ROPE_AXIS_DIM // 2)
    partner = jnp.where(first, pltpu.roll(xh, HEAD_DIM - ROPE_AXIS_DIM // 2, 1),
                        pltpu.roll(xh, ROPE_AXIS_DIM // 2, 1))
    return xh * cos + partner * sin_signed


def _l1_in_kernel(*refs, tile, latent):
    d, e, hd = D_MODEL, D_INNER, HEAD_DIM
    if latent:
        (x_ref, mod_ref, ng_ref, w_ref, qg_ref, kg_ref, cos_ref, sin_ref,
         q_ref, k_ref, v_ref, zs_ref) = refs
    else:
        x_ref, mod_ref, ng_ref, w_ref, kg_ref, k_ref, v_ref = refs
    shift = mod_ref[0, :, 0:d]
    scale = mod_ref[0, :, d:2 * d]
    gmul = ng_ref[...] * (1.0 + scale)
    h = _modulate(x_ref[0], gmul, shift).astype(_BF16)

    if latent:
        cos = cos_ref[...]
        sin = sin_ref[...]
        qg = qg_ref[...]
        for c in range(e // KV_DIM):
            u = _dot(h, w_ref[:, c * KV_DIM:(c + 1) * KV_DIM])
            for i in range(KV_DIM // hd):
                qh = _rope(_head_rmsnorm(u[:, i * hd:(i + 1) * hd], qg), cos, sin)
                q_ref[0, :, c * KV_DIM + i * hd:c * KV_DIM + (i + 1) * hd] = qh.astype(_BF16)
        for c in range(e // KV_DIM):
            z = _dot(h, w_ref[:, e + 2 * KV_DIM + c * KV_DIM:e + 2 * KV_DIM + (c + 1) * KV_DIM])
            zs_ref[0, :, c * KV_DIM:(c + 1) * KV_DIM] = _silu(z).astype(_BF16)

    kg = kg_ref[...]
    ku = _dot(h, w_ref[:, e:e + KV_DIM])
    for i in range(N_KV_HEADS):
        kh = _head_rmsnorm(ku[:, i * hd:(i + 1) * hd], kg)
        if latent:
            kh = _rope(kh, cos, sin)
        k_ref[0, :, i * hd:(i + 1) * hd] = kh.astype(_BF16)
    v_ref[0] = _dot(h, w_ref[:, e + KV_DIM:e + 2 * KV_DIM]).astype(_BF16)


def _layer1_in(x, mod, mod_row, ng, w, qg, kg, cos, sin, *, tile, latent):
    bsz, seq_len, d = x.shape
    e = D_INNER
    nt = seq_len // tile
    ncol = 2 * e + 2 * KV_DIM
    if mod_row is None:
        mod_idx = lambda b, j: (b, 0, 0)
    else:
        mod_idx = lambda b, j: (mod_row, 0, 0)
    tok = lambda width: pl.BlockSpec((1, tile, width), lambda b, j: (b, j, 0))
    in_specs = [tok(d), pl.BlockSpec((1, 1, 3 * d), mod_idx), _const_spec((1, d)),
                _const_spec((d, ncol))]
    args = [x, mod, ng, w]
    kv_shape = jax.ShapeDtypeStruct((bsz, seq_len, KV_DIM), _BF16)
    if latent:
        pos = pl.BlockSpec((tile, HEAD_DIM), lambda b, j: (j, 0))
        in_specs += [_const_spec((1, HEAD_DIM)), _const_spec((1, HEAD_DIM)), pos, pos]
        args += [qg, kg, cos, sin]
        wide = jax.ShapeDtypeStruct((bsz, seq_len, e), _BF16)
        out_specs = [tok(e), tok(KV_DIM), tok(KV_DIM), tok(e)]
        out_shape = [wide, kv_shape, kv_shape, wide]
    else:
        in_specs += [_const_spec((1, HEAD_DIM))]
        args += [kg]
        out_specs = [tok(KV_DIM), tok(KV_DIM)]
        out_shape = [kv_shape, kv_shape]
    return pl.pallas_call(
        functools.partial(_l1_in_kernel, tile=tile, latent=latent),
        grid=(bsz, nt),
        in_specs=in_specs,
        out_specs=out_specs,
        out_shape=out_shape,
        compiler_params=pltpu.CompilerParams(
            dimension_semantics=("parallel", "parallel"), vmem_limit_bytes=VMEM_LIMIT),
        name="layer1_in_latent" if latent else "layer1_in_ctx",
    )(*args)


def _attn_kernel(q_ref, k_ref, v_ref, zs_ref, y_ref, s0_ref, s1_ref, p0_ref, p1_ref, *,
                 tq, tk, n_chunks):
    hd = HEAD_DIM
    q4 = q_ref[0]
    qs = jnp.concatenate([q4[:, i * hd:(i + 1) * hd] for i in range(GROUP)], axis=0)
    rows = GROUP * tq
    s_bufs = (s0_ref, s1_ref)
    p_bufs = (p0_ref, p1_ref)

    def chunk_rows(c):
        if isinstance(c, int):
            return pl.ds(c * tk, tk)
        return pl.ds(pl.multiple_of(c * tk, tk), tk)

    def tick(t, parity, carry):
        m, l, alpha, acc = carry
        static = isinstance(t, int)
        if not static or t + 2 < n_chunks:
            kc = k_ref[0, chunk_rows(t + 2), :]
            s_bufs[parity][...] = lax.dot_general(
                qs, kc, (((1,), (1,)), ((), ())), preferred_element_type=_F32)
        if not static or t >= 0:
            acc = alpha * acc + _dot(p_bufs[parity][...], v_ref[0, chunk_rows(t), :])
        if not static or 0 <= t + 1 < n_chunks:
            s = s_bufs[1 - parity][...]
            m_new = jnp.maximum(m, jnp.max(s, axis=-1, keepdims=True))
            p = jnp.exp2(s - m_new)
            alpha = jnp.exp2(m - m_new)
            l = alpha * l + jnp.sum(p, axis=-1, keepdims=True)
            p_bufs[1 - parity][...] = p.astype(_BF16)
            m = m_new
        return m, l, alpha, acc

    carry = (jnp.full((rows, 1), -jnp.inf, _F32), jnp.zeros((rows, 1), _F32),
             jnp.ones((rows, 1), _F32), jnp.zeros((rows, hd), _F32))
    carry = tick(-2, 0, carry)
    carry = tick(-1, 1, carry)

    n_pairs = (n_chunks - 2) // 2

    def body(i, carry):
        carry = tick(2 * i, 0, carry)
        return tick(2 * i + 1, 1, carry)

    carry = lax.fori_loop(0, n_pairs, body, carry)
    for t in range(2 * n_pairs, n_chunks):
        carry = tick(t, t % 2, carry)
    m, l, alpha, acc = carry
    o = acc / l
    for i in range(GROUP):
        zi = zs_ref[0, :, i * hd:(i + 1) * hd].astype(_F32)
        y_ref[0, :, i * hd:(i + 1) * hd] = (o[i * tq:(i + 1) * tq, :] * zi).astype(_BF16)


def _attn_bounded_kernel(q_ref, k_ref, v_ref, zs_ref, y_ref, p0_ref, p1_ref, *, tq, tk, n_chunks):
    hd = HEAD_DIM
    q4 = q_ref[0]
    qs = jnp.concatenate([q4[:, i * hd:(i + 1) * hd] for i in range(GROUP)], axis=0)
    rows = GROUP * tq
    p_bufs = (p0_ref, p1_ref)
    ones = jnp.ones((tk, hd), _BF16)

    def chunk_rows(c):
        if isinstance(c, int):
            return pl.ds(c * tk, tk)
        return pl.ds(pl.multiple_of(c * tk, tk), tk)

    def tick(t, parity, acc):
        static = isinstance(t, int)
        if not static or t + 1 < n_chunks:
            s = lax.dot_general(qs, k_ref[0, chunk_rows(t + 1), :], (((1,), (1,)), ((), ())),
                                preferred_element_type=_F32)
            p_bufs[1 - parity][...] = jnp.exp2(s).astype(_BF16)
        if not static or t >= 0:
            v_aug = jnp.concatenate([v_ref[0, chunk_rows(t), :], ones], axis=1)
            acc = acc + _dot(p_bufs[parity][...], v_aug)
        return acc

    acc = jnp.zeros((rows, 2 * hd), _F32)
    for t in range(-1, n_chunks):
        acc = tick(t, t % 2, acc)
    o = acc[:, :hd] / acc[:, hd:]
    for i in range(GROUP):
        zi = zs_ref[0, :, i * hd:(i + 1) * hd].astype(_F32)
        y_ref[0, :, i * hd:(i + 1) * hd] = (o[i * tq:(i + 1) * tq, :] * zi).astype(_BF16)


def _attention(q, k_all, v_all, zs, score_bound, *, tq, tk):
    bsz, seq_len, e = q.shape
    kv_len = k_all.shape[1]
    hd = HEAD_DIM
    rows = GROUP * tq
    n_chunks = kv_len // tk
    qspec = pl.BlockSpec((1, tq, KV_DIM), lambda b, g, i: (b, i, g))
    kvspec = pl.BlockSpec((1, kv_len, hd), lambda b, g, i: (b, 0, g))
    common = dict(
        grid=(bsz, N_KV_HEADS, seq_len // tq),
        in_specs=[qspec, kvspec, kvspec, qspec],
        out_specs=qspec,
        out_shape=jax.ShapeDtypeStruct((bsz, seq_len, e), _BF16),
        compiler_params=pltpu.CompilerParams(
            dimension_semantics=("parallel", "parallel", "parallel"), vmem_limit_bytes=VMEM_LIMIT),
    )
    p_scratch = [pltpu.VMEM((rows, tk), _BF16), pltpu.VMEM((rows, tk), _BF16)]
    s_scratch = [pltpu.VMEM((rows, tk), _F32), pltpu.VMEM((rows, tk), _F32)]
    bounded = pl.pallas_call(
        functools.partial(_attn_bounded_kernel, tq=tq, tk=tk, n_chunks=n_chunks),
        scratch_shapes=p_scratch, name="gqa_attention_bounded", **common)
    general = pl.pallas_call(
        functools.partial(_attn_kernel, tq=tq, tk=tk, n_chunks=n_chunks),
        scratch_shapes=s_scratch + p_scratch, name="gqa_attention", **common)
    return lax.cond(score_bound <= MAX_UNSHIFTED_SCORE, bounded, general, q, k_all, v_all, zs)


def _out_kernel(y_ref, x_ref, mod_ref, w_ref, fg_ref, o_ref):
    d = D_MODEL
    gate = mod_ref[0, :, 2 * d:3 * d]
    xo = x_ref[0] + gate * _dot(y_ref[0], w_ref[...])
    ms = jnp.mean(xo * xo, axis=-1, keepdims=True)
    o_ref[0] = xo * lax.rsqrt(ms + RMS_EPS) * fg_ref[...]


def _out_proj(y, x, mod, w, fg, *, tile):
    bsz, seq_len, d = x.shape
    e = D_INNER
    return pl.pallas_call(
        _out_kernel,
        grid=(bsz, seq_len // tile),
        in_specs=[
            pl.BlockSpec((1, tile, e), lambda b, j: (b, j, 0)),
            pl.BlockSpec((1, tile, d), lambda b, j: (b, j, 0)),
            pl.BlockSpec((1, 1, 3 * d), lambda b, j: (b, 0, 0)),
            _const_spec((e, d)),
            _const_spec((1, d)),
        ],
        out_specs=pl.BlockSpec((1, tile, d), lambda b, j: (b, j, 0)),
        out_shape=jax.ShapeDtypeStruct((bsz, seq_len, d), _F32),
        compiler_params=pltpu.CompilerParams(
            dimension_semantics=("parallel", "parallel"), vmem_limit_bytes=VMEM_LIMIT),
        name="out_proj_norm",
    )(y, x, mod, w, fg)


def _rope_tables(length):
    rows = length // GRID_W
    row = jnp.broadcast_to(jnp.arange(rows)[:, None], (rows, GRID_W)).reshape(-1)
    col = jnp.broadcast_to(jnp.arange(GRID_W)[None, :], (rows, GRID_W)).reshape(-1)
    inv_freq = ROPE_THETA ** (-jnp.arange(0, ROPE_AXIS_DIM, 2, dtype=_F32) / ROPE_AXIS_DIM)
    ang_r = row.astype(_F32)[:, None] * inv_freq
    ang_c = col.astype(_F32)[:, None] * inv_freq
    cos = jnp.concatenate([jnp.cos(ang_r)] * 2 + [jnp.cos(ang_c)] * 2, axis=-1)
    sin = jnp.concatenate([-jnp.sin(ang_r), jnp.sin(ang_r), -jnp.sin(ang_c), jnp.sin(ang_c)], axis=-1)
    return cos, sin


def kernel(x, c, ctx, c_ctx, l0_norm_g, l0_ada_w, l0_ada_b, l0_w_in, l0_b_in, l0_dw_w, l0_dw_b, l0_ln_g, l0_ln_b, l0_w_out, l0_b_out, l1_norm_g, l1_ada_w, l1_ada_b, l1_w_in, l1_q_norm_g, l1_k_norm_g, l1_w_out, final_norm_g):
    bsz, seq_len, d = x.shape
    ctx_len = ctx.shape[1]
    row = lambda a: a.reshape(1, -1)

    cond = jnp.zeros((COND_ROWS, d), _F32).at[:bsz].set(c).at[bsz].set(c_ctx)
    mod0, mod1 = _ada(cond, l0_ada_w, l0_ada_b, l1_ada_w, l1_ada_b)
    mod0 = mod0.reshape(COND_ROWS, 1, 3 * d)
    mod1 = mod1.reshape(COND_ROWS, 1, 3 * d)

    l0_args = (row(l0_norm_g), l0_w_in.astype(_BF16), row(l0_b_in), l0_dw_w, row(l0_dw_b),
               row(l0_ln_g), row(l0_ln_b), l0_w_out.astype(_BF16), row(l0_b_out))
    x1 = _layer0(x, mod0, None, *l0_args, tile=512)
    ctx1 = _layer0(ctx, mod0, bsz, *l0_args, tile=ctx_len)

    qg = row(l1_q_norm_g) * (HEAD_DIM ** -0.5 * math.log2(math.e))
    kg = row(l1_k_norm_g)
    w1 = l1_w_in.astype(_BF16)
    cos, sin = _rope_tables(seq_len)
    q, k, v, zs = _layer1_in(x1, mod1, None, row(l1_norm_g), w1, qg, kg, cos, sin,
                             tile=512, latent=True)
    k_ctx, v_ctx = _layer1_in(ctx1, mod1, bsz, row(l1_norm_g), w1, None, kg, None, None,
                              tile=ctx_len, latent=False)

    k_all = jnp.concatenate([k_ctx, k], axis=1)
    v_all = jnp.concatenate([v_ctx, v], axis=1)
    score_bound = HEAD_DIM * jnp.max(jnp.abs(qg)) * jnp.max(jnp.abs(kg)) * SCORE_BOUND_MARGIN
    y = _attention(q, k_all, v_all, zs, score_bound, tq=128, tk=768)
    return _out_proj(y, x1, mod1, l1_w_out.astype(_BF16), row(final_norm_g), tile=512)
```

```python
import functools
import math

import jax
import jax.numpy as jnp
from jax import lax
from jax.experimental import pallas as pl
from jax.experimental.pallas import tpu as pltpu

D_MODEL = 1024
GRID_W = 64
D_INNER = 2048
CONV_WIDTH = 31
HEAD_DIM = 128
N_Q_HEADS = 16
N_KV_HEADS = 4
GROUP = 4
KV_DIM = 512
ROPE_AXIS_DIM = 64
ROPE_THETA = 10000.0
RMS_EPS = 1e-6
LN_EPS = 1e-5

SUBLANES = 8
HALO = 16
COND_ROWS = 16
VMEM_LIMIT = 56 * 1024 * 1024
MAX_UNSHIFTED_SCORE = 64.0
SCORE_BOUND_MARGIN = 1.02

_BF16 = jnp.bfloat16
_F32 = jnp.float32


def _sigmoid(x):
    return 1.0 / (1.0 + jnp.exp(-x))


def _silu(x):
    return x * _sigmoid(x)


def _dot(a, b):
    return jnp.dot(a, b, preferred_element_type=_F32)


def _const_spec(shape):
    nd = len(shape)
    return pl.BlockSpec(shape, lambda *_: (0,) * nd, pipeline_mode=pl.Buffered(1))


def _ada_kernel(cond_ref, w0_ref, b0_ref, w1_ref, b1_ref, o0_ref, o1_ref):
    s = _silu(cond_ref[...])
    o0_ref[...] = jnp.dot(s, w0_ref[...], preferred_element_type=_F32,
                          precision=lax.Precision.HIGHEST) + b0_ref[...]
    o1_ref[...] = jnp.dot(s, w1_ref[...], preferred_element_type=_F32,
                          precision=lax.Precision.HIGHEST) + b1_ref[...]


def _ada(cond, w0, b0, w1, b1):
    d = D_MODEL
    wspec = pl.BlockSpec((d, d), lambda n: (0, n))
    bspec = pl.BlockSpec((1, d), lambda n: (0, n))
    ospec = pl.BlockSpec((COND_ROWS, d), lambda n: (0, n))
    return pl.pallas_call(
        _ada_kernel,
        grid=(3,),
        in_specs=[pl.BlockSpec((COND_ROWS, d), lambda n: (0, 0)), wspec, bspec, wspec, bspec],
        out_specs=[ospec, ospec],
        out_shape=[jax.ShapeDtypeStruct((COND_ROWS, 3 * d), _F32)] * 2,
        compiler_params=pltpu.CompilerParams(vmem_limit_bytes=VMEM_LIMIT),
        name="adaln",
    )(cond, w0, b0.reshape(1, -1), w1, b1.reshape(1, -1))


def _modulate(xf, gmul, shift):
    ms = jnp.mean(xf * xf, axis=-1, keepdims=True)
    return xf * lax.rsqrt(ms + RMS_EPS) * gmul + shift


FIRST_TAP_ROW = HALO - CONV_WIDTH // 2


def _halo_tiles(tile):
    return (tile + 2 * HALO) // SUBLANES


def _l0_kernel(xm_ref, xp_ref, xn_ref, mod_ref, ng_ref, win_ref, bin_ref, dww_ref, dwb_ref,
               lng_ref, lnb_ref, wout_ref, bout_ref, o_ref,
               h_ref, vs0_ref, vs1_ref, conv_ref, *, tile, seq_len, cblk, rblk):
    d, e = D_MODEL, D_INNER
    n_cb = e // cblk
    first_tap_row = FIRST_TAP_ROW
    j = pl.program_id(1)
    shift = mod_ref[0, :, 0:d]
    scale = mod_ref[0, :, d:2 * d]
    gate = mod_ref[0, :, 2 * d:3 * d]
    gmul = ng_ref[...] * (1.0 + scale)

    xm = xm_ref[0]
    h_ref[0:HALO, :] = _modulate(xp_ref[0], gmul, shift).astype(_BF16)
    h_ref[HALO:HALO + tile, :] = _modulate(xm, gmul, shift).astype(_BF16)
    h_ref[HALO + tile:, :] = _modulate(xn_ref[0], gmul, shift).astype(_BF16)

    tok = j * tile - HALO + lax.broadcasted_iota(jnp.int32, (tile + 2 * HALO, 1), 0)
    valid = (tok >= 0) & (tok < seq_len)
    vs_bufs = (vs0_ref, vs1_ref)

    def cols(cb, base=0):
        if isinstance(cb, int):
            return pl.ds(base + cb * cblk, cblk)
        return pl.ds(base + pl.multiple_of(cb * cblk, cblk), cblk)

    def glu_stage(cb, parity):
        hh = h_ref[...]
        a = _dot(hh, win_ref[:, cols(cb)]) + bin_ref[:, cols(cb)]
        g = _dot(hh, win_ref[:, cols(cb, e)]) + bin_ref[:, cols(cb, e)]
        v = jnp.where(valid, a * _sigmoid(g), 0.0)
        n_in = v.shape[0] // SUBLANES
        v3 = v.reshape(n_in, SUBLANES, cblk)
        buf = vs_bufs[parity]
        buf[0] = v3
        for s in range(1, SUBLANES):
            rot = pltpu.roll(v3, SUBLANES - s, 1)
            buf[s, :, 0:SUBLANES - s, :] = rot[:, 0:SUBLANES - s, :]
            buf[s, 0:n_in - 1, SUBLANES - s:, :] = rot[1:, SUBLANES - s:, :]

    def conv_stage(cb, parity):
        rt = rblk // SUBLANES
        for rb in range(tile // rblk):
            acc = jnp.broadcast_to(dwb_ref[:, cols(cb)], (rt, SUBLANES, cblk))
            for k in range(CONV_WIDTH):
                r = first_tap_row + k
                t0 = rb * rt + r // SUBLANES
                acc = acc + vs_bufs[parity][r % SUBLANES, t0:t0 + rt] * dww_ref[k, :, cols(cb)]
            conv_ref[rb * rblk:(rb + 1) * rblk, cols(cb)] = acc.reshape(rblk, cblk)

    glu_stage(0, 0)

    def body(i, carry):
        glu_stage(2 * i + 1, 1)
        conv_stage(2 * i, 0)
        glu_stage(2 * i + 2, 0)
        conv_stage(2 * i + 1, 1)
        return carry

    lax.fori_loop(0, n_cb // 2 - 1, body, 0)
    glu_stage(n_cb - 1, 1)
    conv_stage(n_cb - 2, 0)
    conv_stage(n_cb - 1, 1)

    cv = conv_ref[...]
    mu = jnp.mean(cv, axis=-1, keepdims=True)
    xc = cv - mu
    rstd = lax.rsqrt(jnp.mean(xc * xc, axis=-1, keepdims=True) + LN_EPS)
    h_tile = h_ref[HALO:HALO + tile, :]
    out = jnp.broadcast_to(bout_ref[...], (tile, d))
    for cb in range(n_cb):
        z = _dot(h_tile, win_ref[:, cols(cb, 2 * e)]) + bin_ref[:, cols(cb, 2 * e)]
        y = _silu((conv_ref[:, cols(cb)] - mu) * rstd * lng_ref[:, cols(cb)] + lnb_ref[:, cols(cb)])
        out = out + _dot((y * _silu(z)).astype(_BF16), wout_ref[cb * cblk:(cb + 1) * cblk, :])
    o_ref[0] = xm + gate * out


def _layer0(x, mod, mod_row, ng, win, b_in, dww, dwb, lng, lnb, wout, bout, *, tile):
    bsz, seq_len, d = x.shape
    e = D_INNER
    nt = seq_len // tile
    hb = tile // HALO
    n_hblk = seq_len // HALO
    cblk, rblk = 256, 64
    if mod_row is None:
        mod_idx = lambda b, j: (b, 0, 0)
    else:
        mod_idx = lambda b, j: (mod_row, 0, 0)
    kern = functools.partial(_l0_kernel, tile=tile, seq_len=seq_len, cblk=cblk, rblk=rblk)
    return pl.pallas_call(
        kern,
        grid=(bsz, nt),
        in_specs=[
            pl.BlockSpec((1, tile, d), lambda b, j: (b, j, 0)),
            pl.BlockSpec((1, HALO, d), lambda b, j: (b, jnp.maximum(j * hb - 1, 0), 0)),
            pl.BlockSpec((1, HALO, d), lambda b, j: (b, jnp.minimum((j + 1) * hb, n_hblk - 1), 0)),
            pl.BlockSpec((1, 1, 3 * d), mod_idx),
            _const_spec((1, d)),
            _const_spec((d, 3 * e)),
            _const_spec((1, 3 * e)),
            _const_spec((CONV_WIDTH, SUBLANES, e)),
            _const_spec((1, e)),
            _const_spec((1, e)),
            _const_spec((1, e)),
            _const_spec((e, d)),
            _const_spec((1, d)),
        ],
        out_specs=pl.BlockSpec((1, tile, d), lambda b, j: (b, j, 0)),
        out_shape=jax.ShapeDtypeStruct((bsz, seq_len, d), _F32),
        scratch_shapes=[
            pltpu.VMEM((tile + 2 * HALO, d), _BF16),
            pltpu.VMEM((SUBLANES, _halo_tiles(tile), SUBLANES, cblk), _F32),
            pltpu.VMEM((SUBLANES, _halo_tiles(tile), SUBLANES, cblk), _F32),
            pltpu.VMEM((tile, e), _F32),
        ],
        compiler_params=pltpu.CompilerParams(
            dimension_semantics=("parallel", "parallel"), vmem_limit_bytes=VMEM_LIMIT),
        name="layer0_conv",
    )(x, x, x, mod, ng, win, b_in, dww, dwb, lng, lnb, wout, bout)


def _head_rmsnorm(xh, g):
    ms = jnp.mean(xh * xh, axis=-1, keepdims=True)
    return xh * lax.rsqrt(ms + RMS_EPS) * g


def _rope(xh, cos, sin_signed):
    lane = lax.broadcasted_iota(jnp.int32, xh.shape, 1)
    first = (lane % ROPE_AXIS_DIM) < (ROPE_AXIS_DIM // 2)
    partner = jnp.where(first, pltpu.roll(xh, HEAD_DIM - ROPE_AXIS_DIM // 2, 1),
                        pltpu.roll(xh, ROPE_AXIS_DIM // 2, 1))
    return xh * cos + partner * sin_signed


def _l1_in_kernel(*refs, tile, latent):
    d, e, hd = D_MODEL, D_INNER, HEAD_DIM
    if latent:
        (x_ref, mod_ref, ng_ref, w_ref, qg_ref, kg_ref, cos_ref, sin_ref,
         q_ref, k_ref, v_ref, zs_ref) = refs
    else:
        x_ref, mod_ref, ng_ref, w_ref, kg_ref, k_ref, v_ref = refs
    shift = mod_ref[0, :, 0:d]
    scale = mod_ref[0, :, d:2 * d]
    gmul = ng_ref[...] * (1.0 + scale)
    h = _modulate(x_ref[0], gmul, shift).astype(_BF16)

    if latent:
        cos = cos_ref[...]
        sin = sin_ref[...]
        qg = qg_ref[...]
        for c in range(e // KV_DIM):
            u = _dot(h, w_ref[:, c * KV_DIM:(c + 1) * KV_DIM])
            for i in range(KV_DIM // hd):
                qh = _rope(_head_rmsnorm(u[:, i * hd:(i + 1) * hd], qg), cos, sin)
                q_ref[0, :, c * KV_DIM + i * hd:c * KV_DIM + (i + 1) * hd] = qh.astype(_BF16)
        for c in range(e // KV_DIM):
            z = _dot(h, w_ref[:, e + 2 * KV_DIM + c * KV_DIM:e + 2 * KV_DIM + (c + 1) * KV_DIM])
            zs_ref[0, :, c * KV_DIM:(c + 1) * KV_DIM] = _silu(z).astype(_BF16)

    kg = kg_ref[...]
    ku = _dot(h, w_ref[:, e:e + KV_DIM])
    for i in range(N_KV_HEADS):
        kh = _head_rmsnorm(ku[:, i * hd:(i + 1) * hd], kg)
        if latent:
            kh = _rope(kh, cos, sin)
        k_ref[0, :, i * hd:(i + 1) * hd] = kh.astype(_BF16)
    v_ref[0] = _dot(h, w_ref[:, e + KV_DIM:e + 2 * KV_DIM]).astype(_BF16)


def _layer1_in(x, mod, mod_row, ng, w, qg, kg, cos, sin, *, tile, latent):
    bsz, seq_len, d = x.shape
    e = D_INNER
    nt = seq_len // tile
    ncol = 2 * e + 2 * KV_DIM
    if mod_row is None:
        mod_idx = lambda b, j: (b, 0, 0)
    else:
        mod_idx = lambda b, j: (mod_row, 0, 0)
    tok = lambda width: pl.BlockSpec((1, tile, width), lambda b, j: (b, j, 0))
    in_specs = [tok(d), pl.BlockSpec((1, 1, 3 * d), mod_idx), _const_spec((1, d)),
                _const_spec((d, ncol))]
    args = [x, mod, ng, w]
    kv_shape = jax.ShapeDtypeStruct((bsz, seq_len, KV_DIM), _BF16)
    if latent:
        pos = pl.BlockSpec((tile, HEAD_DIM), lambda b, j: (j, 0))
        in_specs += [_const_spec((1, HEAD_DIM)), _const_spec((1, HEAD_DIM)), pos, pos]
        args += [qg, kg, cos, sin]
        wide = jax.ShapeDtypeStruct((bsz, seq_len, e), _BF16)
        out_specs = [tok(e), tok(KV_DIM), tok(KV_DIM), tok(e)]
        out_shape = [wide, kv_shape, kv_shape, wide]
    else:
        in_specs += [_const_spec((1, HEAD_DIM))]
        args += [kg]
        out_specs = [tok(KV_DIM), tok(KV_DIM)]
        out_shape = [kv_shape, kv_shape]
    return pl.pallas_call(
        functools.partial(_l1_in_kernel, tile=tile, latent=latent),
        grid=(bsz, nt),
        in_specs=in_specs,
        out_specs=out_specs,
        out_shape=out_shape,
        compiler_params=pltpu.CompilerParams(
            dimension_semantics=("parallel", "parallel"), vmem_limit_bytes=VMEM_LIMIT),
        name="layer1_in_latent" if latent else "layer1_in_ctx",
    )(*args)


def _attn_kernel(q_ref, k_ref, v_ref, zs_ref, y_ref, s0_ref, s1_ref, p0_ref, p1_ref, *,
                 tq, tk, n_chunks):
    hd = HEAD_DIM
    q4 = q_ref[0]
    qs = jnp.concatenate([q4[:, i * hd:(i + 1) * hd] for i in range(GROUP)], axis=0)
    rows = GROUP * tq
    s_bufs = (s0_ref, s1_ref)
    p_bufs = (p0_ref, p1_ref)

    def chunk_rows(c):
        if isinstance(c, int):
            return pl.ds(c * tk, tk)
        return pl.ds(pl.multiple_of(c * tk, tk), tk)

    def tick(t, parity, carry):
        m, l, alpha, acc = carry
        static = isinstance(t, int)
        if not static or t + 2 < n_chunks:
            kc = k_ref[0, chunk_rows(t + 2), :]
            s_bufs[parity][...] = lax.dot_general(
                qs, kc, (((1,), (1,)), ((), ())), preferred_element_type=_F32)
        if not static or t >= 0:
            acc = alpha * acc + _dot(p_bufs[parity][...], v_ref[0, chunk_rows(t), :])
        if not static or 0 <= t + 1 < n_chunks:
            s = s_bufs[1 - parity][...]
            m_new = jnp.maximum(m, jnp.max(s, axis=-1, keepdims=True))
            p = jnp.exp2(s - m_new)
            alpha = jnp.exp2(m - m_new)
            l = alpha * l + jnp.sum(p, axis=-1, keepdims=True)
            p_bufs[1 - parity][...] = p.astype(_BF16)
            m = m_new
        return m, l, alpha, acc

    carry = (jnp.full((rows, 1), -jnp.inf, _F32), jnp.zeros((rows, 1), _F32),
             jnp.ones((rows, 1), _F32), jnp.zeros((rows, hd), _F32))
    carry = tick(-2, 0, carry)
    carry = tick(-1, 1, carry)

    n_pairs = (n_chunks - 2) // 2

    def body(i, carry):
        carry = tick(2 * i, 0, carry)
        return tick(2 * i + 1, 1, carry)

    carry = lax.fori_loop(0, n_pairs, body, carry)
    for t in range(2 * n_pairs, n_chunks):
        carry = tick(t, t % 2, carry)
    m, l, alpha, acc = carry
    o = acc / l
    for i in range(GROUP):
        zi = zs_ref[0, :, i * hd:(i + 1) * hd].astype(_F32)
        y_ref[0, :, i * hd:(i + 1) * hd] = (o[i * tq:(i + 1) * tq, :] * zi).astype(_BF16)


def _attn_bounded_kernel(q_ref, k_ref, v_ref, zs_ref, y_ref, p0_ref, p1_ref, *, tq, tk, n_chunks):
    hd = HEAD_DIM
    q4 = q_ref[0]
    qs = jnp.concatenate([q4[:, i * hd:(i + 1) * hd] for i in range(GROUP)], axis=0)
    rows = GROUP * tq
    p_bufs = (p0_ref, p1_ref)
    ones = jnp.ones((tk, hd), _BF16)

    def chunk_rows(c):
        if isinstance(c, int):
            return pl.ds(c * tk, tk)
        return pl.ds(pl.multiple_of(c * tk, tk), tk)

    def tick(t, parity, acc):
        static = isinstance(t, int)
        if not static or t + 1 < n_chunks:
            s = lax.dot_general(qs, k_ref[0, chunk_rows(t + 1), :], (((1,), (1,)), ((), ())),
                                preferred_element_type=_F32)
            p_bufs[1 - parity][...] = jnp.exp2(s).astype(_BF16)
        if not static or t >= 0:
            v_aug = jnp.concatenate([v_ref[0, chunk_rows(t), :], ones], axis=1)
            acc = acc + _dot(p_bufs[parity][...], v_aug)
        return acc

    acc = jnp.zeros((rows, 2 * hd), _F32)
    for t in range(-1, n_chunks):
        acc = tick(t, t % 2, acc)
    o = acc[:, :hd] / acc[:, hd:]
    for i in range(GROUP):
        zi = zs_ref[0, :, i * hd:(i + 1) * hd].astype(_F32)
        y_ref[0, :, i * hd:(i + 1) * hd] = (o[i * tq:(i + 1) * tq, :] * zi).astype(_BF16)


def _attention(q, k_all, v_all, zs, score_bound, *, tq, tk):
    bsz, seq_len, e = q.shape
    kv_len = k_all.shape[1]
    hd = HEAD_DIM
    rows = GROUP * tq
    n_chunks = kv_len // tk
    qspec = pl.BlockSpec((1, tq, KV_DIM), lambda b, g, i: (b, i, g))
    kvspec = pl.BlockSpec((1, kv_len, hd), lambda b, g, i: (b, 0, g))
    common = dict(
        grid=(bsz, N_KV_HEADS, seq_len // tq),
        in_specs=[qspec, kvspec, kvspec, qspec],
        out_specs=qspec,
        out_shape=jax.ShapeDtypeStruct((bsz, seq_len, e), _BF16),
        compiler_params=pltpu.CompilerParams(
            dimension_semantics=("parallel", "parallel", "parallel"), vmem_limit_bytes=VMEM_LIMIT),
    )
    p_scratch = [pltpu.VMEM((rows, tk), _BF16), pltpu.VMEM((rows, tk), _BF16)]
    s_scratch = [pltpu.VMEM((rows, tk), _F32), pltpu.VMEM((rows, tk), _F32)]
    bounded = pl.pallas_call(
        functools.partial(_attn_bounded_kernel, tq=tq, tk=tk, n_chunks=n_chunks),
        scratch_shapes=p_scratch, name="gqa_attention_bounded", **common)
    general = pl.pallas_call(
        functools.partial(_attn_kernel, tq=tq, tk=tk, n_chunks=n_chunks),
        scratch_shapes=s_scratch + p_scratch, name="gqa_attention", **common)
    return lax.cond(score_bound <= MAX_UNSHIFTED_SCORE, bounded, general, q, k_all, v_all, zs)


def _out_kernel(y_ref, x_ref, mod_ref, w_ref, fg_ref, o_ref):
    d = D_MODEL
    gate = mod_ref[0, :, 2 * d:3 * d]
    xo = x_ref[0] + gate * _dot(y_ref[0], w_ref[...])
    ms = jnp.mean(xo * xo, axis=-1, keepdims=True)
    o_ref[0] = xo * lax.rsqrt(ms + RMS_EPS) * fg_ref[...]


def _out_proj(y, x, mod, w, fg, *, tile):
    bsz, seq_len, d = x.shape
    e = D_INNER
    return pl.pallas_call(
        _out_kernel,
        grid=(bsz, seq_len // tile),
        in_specs=[
            pl.BlockSpec((1, tile, e), lambda b, j: (b, j, 0)),
            pl.BlockSpec((1, tile, d), lambda b, j: (b, j, 0)),
            pl.BlockSpec((1, 1, 3 * d), lambda b, j: (b, 0, 0)),
            _const_spec((e, d)),
            _const_spec((1, d)),
        ],
        out_specs=pl.BlockSpec((1, tile, d), lambda b, j: (b, j, 0)),
        out_shape=jax.ShapeDtypeStruct((bsz, seq_len, d), _F32),
        compiler_params=pltpu.CompilerParams(
            dimension_semantics=("parallel", "parallel"), vmem_limit_bytes=VMEM_LIMIT),
        name="out_proj_norm",
    )(y, x, mod, w, fg)


def _rope_tables(length):
    rows = length // GRID_W
    row = jnp.broadcast_to(jnp.arange(rows)[:, None], (rows, GRID_W)).reshape(-1)
    col = jnp.broadcast_to(jnp.arange(GRID_W)[None, :], (rows, GRID_W)).reshape(-1)
    inv_freq = ROPE_THETA ** (-jnp.arange(0, ROPE_AXIS_DIM, 2, dtype=_F32) / ROPE_AXIS_DIM)
    ang_r = row.astype(_F32)[:, None] * inv_freq
    ang_c = col.astype(_F32)[:, None] * inv_freq
    cos = jnp.concatenate([jnp.cos(ang_r)] * 2 + [jnp.cos(ang_c)] * 2, axis=-1)
    sin = jnp.concatenate([-jnp.sin(ang_r), jnp.sin(ang_r), -jnp.sin(ang_c), jnp.sin(ang_c)], axis=-1)
    return cos, sin


def kernel(x, c, ctx, c_ctx, l0_norm_g, l0_ada_w, l0_ada_b, l0_w_in, l0_b_in, l0_dw_w, l0_dw_b, l0_ln_g, l0_ln_b, l0_w_out, l0_b_out, l1_norm_g, l1_ada_w, l1_ada_b, l1_w_in, l1_q_norm_g, l1_k_norm_g, l1_w_out, final_norm_g):
    bsz, seq_len, d = x.shape
    ctx_len = ctx.shape[1]
    row = lambda a: a.reshape(1, -1)

    cond = jnp.zeros((COND_ROWS, d), _F32).at[:bsz].set(c).at[bsz].set(c_ctx)
    mod0, mod1 = _ada(cond, l0_ada_w, l0_ada_b, l1_ada_w, l1_ada_b)
    mod0 = mod0.reshape(COND_ROWS, 1, 3 * d)
    mod1 = mod1.reshape(COND_ROWS, 1, 3 * d)

    dww = jnp.broadcast_to(l0_dw_w[:, None, :], (CONV_WIDTH, SUBLANES, D_INNER))
    l0_args = (row(l0_norm_g), l0_w_in.astype(_BF16), row(l0_b_in), dww, row(l0_dw_b),
               row(l0_ln_g), row(l0_ln_b), l0_w_out.astype(_BF16), row(l0_b_out))
    x1 = _layer0(x, mod0, None, *l0_args, tile=512)
    ctx1 = _layer0(ctx, mod0, bsz, *l0_args, tile=ctx_len)

    qg = row(l1_q_norm_g) * (HEAD_DIM ** -0.5 * math.log2(math.e))
    kg = row(l1_k_norm_g)
    w1 = l1_w_in.astype(_BF16)
    cos, sin = _rope_tables(seq_len)
    q, k, v, zs = _layer1_in(x1, mod1, None, row(l1_norm_g), w1, qg, kg, cos, sin,
                             tile=512, latent=True)
    k_ctx, v_ctx = _layer1_in(ctx1, mod1, bsz, row(l1_norm_g), w1, None, kg, None, None,
                              tile=ctx_len, latent=False)

    k_all = jnp.concatenate([k_ctx, k], axis=1)
    v_all = jnp.concatenate([v_ctx, v], axis=1)
    score_bound = HEAD_DIM * jnp.max(jnp.abs(qg)) * jnp.max(jnp.abs(kg)) * SCORE_BOUND_MARGIN
    y = _attention(q, k_all, v_all, zs, score_bound, tq=256, tk=768)
    return _out_proj(y, x1, mod1, l1_w_out.astype(_BF16), row(final_norm_g), tile=512)
```

```python
import functools
import math

import jax
import jax.numpy as jnp
from jax import lax
from jax.experimental import pallas as pl
from jax.experimental.pallas import tpu as pltpu

D_MODEL = 1024
GRID_W = 64
D_INNER = 2048
CONV_WIDTH = 31
HEAD_DIM = 128
N_Q_HEADS = 16
N_KV_HEADS = 4
GROUP = 4
KV_DIM = 512
ROPE_AXIS_DIM = 64
ROPE_THETA = 10000.0
RMS_EPS = 1e-6
LN_EPS = 1e-5

SUBLANES = 8
HALO = 16
COND_ROWS = 16
VMEM_LIMIT = 56 * 1024 * 1024
MAX_UNSHIFTED_SCORE = 64.0
SCORE_BOUND_MARGIN = 1.02

_BF16 = jnp.bfloat16
_F32 = jnp.float32


def _sigmoid(x):
    return 1.0 / (1.0 + jnp.exp(-x))


def _silu(x):
    return x * _sigmoid(x)


def _dot(a, b):
    return jnp.dot(a, b, preferred_element_type=_F32)


def _const_spec(shape):
    nd = len(shape)
    return pl.BlockSpec(shape, lambda *_: (0,) * nd, pipeline_mode=pl.Buffered(1))


def _ada_kernel(cond_ref, w0_ref, b0_ref, w1_ref, b1_ref, o0_ref, o1_ref):
    s = _silu(cond_ref[...])
    o0_ref[...] = jnp.dot(s, w0_ref[...], preferred_element_type=_F32,
                          precision=lax.Precision.HIGHEST) + b0_ref[...]
    o1_ref[...] = jnp.dot(s, w1_ref[...], preferred_element_type=_F32,
                          precision=lax.Precision.HIGHEST) + b1_ref[...]


def _ada(cond, w0, b0, w1, b1):
    d = D_MODEL
    wspec = pl.BlockSpec((d, d), lambda n: (0, n))
    bspec = pl.BlockSpec((1, d), lambda n: (0, n))
    ospec = pl.BlockSpec((COND_ROWS, d), lambda n: (0, n))
    return pl.pallas_call(
        _ada_kernel,
        grid=(3,),
        in_specs=[pl.BlockSpec((COND_ROWS, d), lambda n: (0, 0)), wspec, bspec, wspec, bspec],
        out_specs=[ospec, ospec],
        out_shape=[jax.ShapeDtypeStruct((COND_ROWS, 3 * d), _F32)] * 2,
        compiler_params=pltpu.CompilerParams(vmem_limit_bytes=VMEM_LIMIT),
        name="adaln",
    )(cond, w0, b0.reshape(1, -1), w1, b1.reshape(1, -1))


def _modulate(xf, gmul, shift):
    ms = jnp.mean(xf * xf, axis=-1, keepdims=True)
    return xf * lax.rsqrt(ms + RMS_EPS) * gmul + shift


FIRST_TAP_ROW = HALO - CONV_WIDTH // 2

def _halo_tiles(tile):
    return (tile + 2 * HALO) // SUBLANES


def _l0_kernel(xm_ref, xp_ref, xn_ref, mod_ref, ng_ref, win_ref, bin_ref, dww_ref, dwb_ref,
               lng_ref, lnb_ref, wout_ref, bout_ref, o_ref,
               h_ref, vs0_ref, vs1_ref, conv_ref, *, tile, seq_len, cblk, rblk):
    d, e = D_MODEL, D_INNER
    n_cb = e // cblk
    first_tap_row = FIRST_TAP_ROW
    j = pl.program_id(1)
    shift = mod_ref[0, :, 0:d]
    scale = mod_ref[0, :, d:2 * d]
    gate = mod_ref[0, :, 2 * d:3 * d]
    gmul = ng_ref[...] * (1.0 + scale)

    xm = xm_ref[0]
    h_ref[0:HALO, :] = _modulate(xp_ref[0], gmul, shift).astype(_BF16)
    h_ref[HALO:HALO + tile, :] = _modulate(xm, gmul, shift).astype(_BF16)
    h_ref[HALO + tile:, :] = _modulate(xn_ref[0], gmul, shift).astype(_BF16)

    tok = j * tile - HALO + lax.broadcasted_iota(jnp.int32, (tile + 2 * HALO, 1), 0)
    valid = (tok >= 0) & (tok < seq_len)
    vs_bufs = (vs0_ref, vs1_ref)

    def cols(cb, base=0):
        if isinstance(cb, int):
            return pl.ds(base + cb * cblk, cblk)
        return pl.ds(base + pl.multiple_of(cb * cblk, cblk), cblk)

    def glu_stage(cb, parity):
        hh = h_ref[...]
        a = _dot(hh, win_ref[:, cols(cb)]) + bin_ref[:, cols(cb)]
        g = _dot(hh, win_ref[:, cols(cb, e)]) + bin_ref[:, cols(cb, e)]
        v = jnp.where(valid, a * _sigmoid(g), 0.0)
        n_in = v.shape[0] // SUBLANES
        v3 = v.reshape(n_in, SUBLANES, cblk)
        buf = vs_bufs[parity]
        buf[0] = v3
        for s in range(1, SUBLANES):
            rot = pltpu.roll(v3, SUBLANES - s, 1)
            buf[s, :, 0:SUBLANES - s, :] = rot[:, 0:SUBLANES - s, :]
            buf[s, 0:n_in - 1, SUBLANES - s:, :] = rot[1:, SUBLANES - s:, :]

    def conv_stage(cb, parity):
        rt = rblk // SUBLANES
        for rb in range(tile // rblk):
            acc = jnp.broadcast_to(dwb_ref[:, cols(cb)], (rt, SUBLANES, cblk))
            for k in range(CONV_WIDTH):
                r = first_tap_row + k
                t0 = rb * rt + r // SUBLANES
                acc = acc + vs_bufs[parity][r % SUBLANES, t0:t0 + rt] * dww_ref[k, :, cols(cb)]
            conv_ref[rb * rblk:(rb + 1) * rblk, cols(cb)] = acc.reshape(rblk, cblk)

    glu_stage(0, 0)

    def body(i, carry):
        glu_stage(2 * i + 1, 1)
        conv_stage(2 * i, 0)
        glu_stage(2 * i + 2, 0)
        conv_stage(2 * i + 1, 1)
        return carry

    lax.fori_loop(0, n_cb // 2 - 1, body, 0)
    glu_stage(n_cb - 1, 1)
    conv_stage(n_cb - 2, 0)
    conv_stage(n_cb - 1, 1)

    cv = conv_ref[...]
    mu = jnp.mean(cv, axis=-1, keepdims=True)
    xc = cv - mu
    rstd = lax.rsqrt(jnp.mean(xc * xc, axis=-1, keepdims=True) + LN_EPS)
    h_tile = h_ref[HALO:HALO + tile, :]
    out = jnp.broadcast_to(bout_ref[...], (tile, d))
    for cb in range(n_cb):
        z = _dot(h_tile, win_ref[:, cols(cb, 2 * e)]) + bin_ref[:, cols(cb, 2 * e)]
        y = _silu((conv_ref[:, cols(cb)] - mu) * rstd * lng_ref[:, cols(cb)] + lnb_ref[:, cols(cb)])
        out = out + _dot((y * _silu(z)).astype(_BF16), wout_ref[cb * cblk:(cb + 1) * cblk, :])
    o_ref[0] = xm + gate * out


def _layer0(x, mod, mod_row, ng, win, b_in, dww, dwb, lng, lnb, wout, bout, *, tile):
    bsz, seq_len, d = x.shape
    e = D_INNER
    nt = seq_len // tile
    hb = tile // HALO
    n_hblk = seq_len // HALO
    cblk, rblk = 256, 64
    if mod_row is None:
        mod_idx = lambda b, j: (b, 0, 0)
    else:
        mod_idx = lambda b, j: (mod_row, 0, 0)
    kern = functools.partial(_l0_kernel, tile=tile, seq_len=seq_len, cblk=cblk, rblk=rblk)
    return pl.pallas_call(
        kern,
        grid=(bsz, nt),
        in_specs=[
            pl.BlockSpec((1, tile, d), lambda b, j: (b, j, 0)),
            pl.BlockSpec((1, HALO, d), lambda b, j: (b, jnp.maximum(j * hb - 1, 0), 0)),
            pl.BlockSpec((1, HALO, d), lambda b, j: (b, jnp.minimum((j + 1) * hb, n_hblk - 1), 0)),
            pl.BlockSpec((1, 1, 3 * d), mod_idx),
            _const_spec((1, d)),
            _const_spec((d, 3 * e)),
            _const_spec((1, 3 * e)),
            _const_spec((CONV_WIDTH, SUBLANES, e)),
            _const_spec((1, e)),
            _const_spec((1, e)),
            _const_spec((1, e)),
            _const_spec((e, d)),
            _const_spec((1, d)),
        ],
        out_specs=pl.BlockSpec((1, tile, d), lambda b, j: (b, j, 0)),
        out_shape=jax.ShapeDtypeStruct((bsz, seq_len, d), _F32),
        scratch_shapes=[
            pltpu.VMEM((tile + 2 * HALO, d), _BF16),
            pltpu.VMEM((SUBLANES, _halo_tiles(tile), SUBLANES, cblk), _F32),
            pltpu.VMEM((SUBLANES, _halo_tiles(tile), SUBLANES, cblk), _F32),
            pltpu.VMEM((tile, e), _F32),
        ],
        compiler_params=pltpu.CompilerParams(
            dimension_semantics=("parallel", "parallel"), vmem_limit_bytes=VMEM_LIMIT),
        name="layer0_conv",
    )(x, x, x, mod, ng, win, b_in, dww, dwb, lng, lnb, wout, bout)


def _head_rmsnorm(xh, g):
    ms = jnp.mean(xh * xh, axis=-1, keepdims=True)
    return xh * lax.rsqrt(ms + RMS_EPS) * g


def _rope(xh, cos, sin_signed):
    lane = lax.broadcasted_iota(jnp.int32, xh.shape, 1)
    first = (lane % ROPE_AXIS_DIM) < (ROPE_AXIS_DIM // 2)
    partner = jnp.where(first, pltpu.roll(xh, HEAD_DIM - ROPE_AXIS_DIM // 2, 1),
                        pltpu.roll(xh, ROPE_AXIS_DIM // 2, 1))
    return xh * cos + partner * sin_signed


def _l1_in_kernel(*refs, tile, latent):
    d, e, hd = D_MODEL, D_INNER, HEAD_DIM
    if latent:
        (x_ref, mod_ref, ng_ref, w_ref, qg_ref, kg_ref, cos_ref, sin_ref,
         q_ref, k_ref, v_ref, zs_ref) = refs
    else:
        x_ref, mod_ref, ng_ref, w_ref, kg_ref, k_ref, v_ref = refs
    shift = mod_ref[0, :, 0:d]
    scale = mod_ref[0, :, d:2 * d]
    gmul = ng_ref[...] * (1.0 + scale)
    h = _modulate(x_ref[0], gmul, shift).astype(_BF16)

    if latent:
        cos = cos_ref[...]
        sin = sin_ref[...]
        qg = qg_ref[...]
        for c in range(e // KV_DIM):
            u = _dot(h, w_ref[:, c * KV_DIM:(c + 1) * KV_DIM])
            for i in range(KV_DIM // hd):
                qh = _rope(_head_rmsnorm(u[:, i * hd:(i + 1) * hd], qg), cos, sin)
                q_ref[0, :, c * KV_DIM + i * hd:c * KV_DIM + (i + 1) * hd] = qh.astype(_BF16)
        for c in range(e // KV_DIM):
            z = _dot(h, w_ref[:, e + 2 * KV_DIM + c * KV_DIM:e + 2 * KV_DIM + (c + 1) * KV_DIM])
            zs_ref[0, :, c * KV_DIM:(c + 1) * KV_DIM] = _silu(z).astype(_BF16)

    kg = kg_ref[...]
    ku = _dot(h, w_ref[:, e:e + KV_DIM])
    for i in range(N_KV_HEADS):
        kh = _head_rmsnorm(ku[:, i * hd:(i + 1) * hd], kg)
        if latent:
            kh = _rope(kh, cos, sin)
        k_ref[0, :, i * hd:(i + 1) * hd] = kh.astype(_BF16)
    v_ref[0] = _dot(h, w_ref[:, e + KV_DIM:e + 2 * KV_DIM]).astype(_BF16)


def _layer1_in(x, mod, mod_row, ng, w, qg, kg, cos, sin, *, tile, latent):
    bsz, seq_len, d = x.shape
    e = D_INNER
    nt = seq_len // tile
    ncol = 2 * e + 2 * KV_DIM
    if mod_row is None:
        mod_idx = lambda b, j: (b, 0, 0)
    else:
        mod_idx = lambda b, j: (mod_row, 0, 0)
    tok = lambda width: pl.BlockSpec((1, tile, width), lambda b, j: (b, j, 0))
    in_specs = [tok(d), pl.BlockSpec((1, 1, 3 * d), mod_idx), _const_spec((1, d)),
                _const_spec((d, ncol))]
    args = [x, mod, ng, w]
    kv_shape = jax.ShapeDtypeStruct((bsz, seq_len, KV_DIM), _BF16)
    if latent:
        pos = pl.BlockSpec((tile, HEAD_DIM), lambda b, j: (j, 0))
        in_specs += [_const_spec((1, HEAD_DIM)), _const_spec((1, HEAD_DIM)), pos, pos]
        args += [qg, kg, cos, sin]
        wide = jax.ShapeDtypeStruct((bsz, seq_len, e), _BF16)
        out_specs = [tok(e), tok(KV_DIM), tok(KV_DIM), tok(e)]
        out_shape = [wide, kv_shape, kv_shape, wide]
    else:
        in_specs += [_const_spec((1, HEAD_DIM))]
        args += [kg]
        out_specs = [tok(KV_DIM), tok(KV_DIM)]
        out_shape = [kv_shape, kv_shape]
    return pl.pallas_call(
        functools.partial(_l1_in_kernel, tile=tile, latent=latent),
        grid=(bsz, nt),
        in_specs=in_specs,
        out_specs=out_specs,
        out_shape=out_shape,
        compiler_params=pltpu.CompilerParams(
            dimension_semantics=("parallel", "parallel"), vmem_limit_bytes=VMEM_LIMIT),
        name="layer1_in_latent" if latent else "layer1_in_ctx",
    )(*args)


def _attn_kernel(q_ref, k_ref, v_ref, zs_ref, y_ref, s0_ref, s1_ref, p0_ref, p1_ref, *,
                 tq, tk, n_chunks):
    hd = HEAD_DIM
    q4 = q_ref[0]
    qs = jnp.concatenate([q4[:, i * hd:(i + 1) * hd] for i in range(GROUP)], axis=0)
    rows = GROUP * tq
    s_bufs = (s0_ref, s1_ref)
    p_bufs = (p0_ref, p1_ref)

    def chunk_rows(c):
        if isinstance(c, int):
            return pl.ds(c * tk, tk)
        return pl.ds(pl.multiple_of(c * tk, tk), tk)

    def tick(t, parity, carry):
        m, l, alpha, acc = carry
        static = isinstance(t, int)
        if not static or t + 2 < n_chunks:
            kc = k_ref[0, chunk_rows(t + 2), :]
            s_bufs[parity][...] = lax.dot_general(
                qs, kc, (((1,), (1,)), ((), ())), preferred_element_type=_F32)
        if not static or t >= 0:
            acc = alpha * acc + _dot(p_bufs[parity][...], v_ref[0, chunk_rows(t), :])
        if not static or 0 <= t + 1 < n_chunks:
            s = s_bufs[1 - parity][...]
            m_new = jnp.maximum(m, jnp.max(s, axis=-1, keepdims=True))
            p = jnp.exp2(s - m_new)
            alpha = jnp.exp2(m - m_new)
            l = alpha * l + jnp.sum(p, axis=-1, keepdims=True)
            p_bufs[1 - parity][...] = p.astype(_BF16)
            m = m_new
        return m, l, alpha, acc

    carry = (jnp.full((rows, 1), -jnp.inf, _F32), jnp.zeros((rows, 1), _F32),
             jnp.ones((rows, 1), _F32), jnp.zeros((rows, hd), _F32))
    carry = tick(-2, 0, carry)
    carry = tick(-1, 1, carry)

    n_pairs = (n_chunks - 2) // 2

    def body(i, carry):
        carry = tick(2 * i, 0, carry)
        return tick(2 * i + 1, 1, carry)

    carry = lax.fori_loop(0, n_pairs, body, carry)
    for t in range(2 * n_pairs, n_chunks):
        carry = tick(t, t % 2, carry)
    m, l, alpha, acc = carry
    o = acc / l
    for i in range(GROUP):
        zi = zs_ref[0, :, i * hd:(i + 1) * hd].astype(_F32)
        y_ref[0, :, i * hd:(i + 1) * hd] = (o[i * tq:(i + 1) * tq, :] * zi).astype(_BF16)


def _attn_bounded_kernel(q_ref, k_ref, v_ref, zs_ref, y_ref, p0_ref, p1_ref, *, tq, tk, n_chunks):
    hd = HEAD_DIM
    q4 = q_ref[0]
    qs = jnp.concatenate([q4[:, i * hd:(i + 1) * hd] for i in range(GROUP)], axis=0)
    rows = GROUP * tq
    p_bufs = (p0_ref, p1_ref)
    ones = jnp.ones((tk, hd), _BF16)

    def chunk_rows(c):
        if isinstance(c, int):
            return pl.ds(c * tk, tk)
        return pl.ds(pl.multiple_of(c * tk, tk), tk)

    def tick(t, parity, acc):
        static = isinstance(t, int)
        if not static or t + 1 < n_chunks:
            s = lax.dot_general(qs, k_ref[0, chunk_rows(t + 1), :], (((1,), (1,)), ((), ())),
                                preferred_element_type=_F32)
            p_bufs[1 - parity][...] = jnp.exp2(s).astype(_BF16)
        if not static or t >= 0:
            v_aug = jnp.concatenate([v_ref[0, chunk_rows(t), :], ones], axis=1)
            acc = acc + _dot(p_bufs[parity][...], v_aug)
        return acc

    acc = jnp.zeros((rows, 2 * hd), _F32)
    for t in range(-1, n_chunks):
        acc = tick(t, t % 2, acc)
    o = acc[:, :hd] / acc[:, hd:]
    for i in range(GROUP):
        zi = zs_ref[0, :, i * hd:(i + 1) * hd].astype(_F32)
        y_ref[0, :, i * hd:(i + 1) * hd] = (o[i * tq:(i + 1) * tq, :] * zi).astype(_BF16)


def _attention(q, k_all, v_all, zs, score_bound, *, tq, tk):
    bsz, seq_len, e = q.shape
    kv_len = k_all.shape[1]
    hd = HEAD_DIM
    rows = GROUP * tq
    n_chunks = kv_len // tk
    qspec = pl.BlockSpec((1, tq, KV_DIM), lambda b, g, i: (b, i, g))
    kvspec = pl.BlockSpec((1, kv_len, hd), lambda b, g, i: (b, 0, g))
    common = dict(
        grid=(bsz, N_KV_HEADS, seq_len // tq),
        in_specs=[qspec, kvspec, kvspec, qspec],
        out_specs=qspec,
        out_shape=jax.ShapeDtypeStruct((bsz, seq_len, e), _BF16),
        compiler_params=pltpu.CompilerParams(
            dimension_semantics=("parallel", "parallel", "parallel"), vmem_limit_bytes=VMEM_LIMIT),
    )
    p_scratch = [pltpu.VMEM((rows, tk), _BF16), pltpu.VMEM((rows, tk), _BF16)]
    s_scratch = [pltpu.VMEM((rows, tk), _F32), pltpu.VMEM((rows, tk), _F32)]
    bounded = pl.pallas_call(
        functools.partial(_attn_bounded_kernel, tq=tq, tk=tk, n_chunks=n_chunks),
        scratch_shapes=p_scratch, name="gqa_attention_bounded", **common)
    general = pl.pallas_call(
        functools.partial(_attn_kernel, tq=tq, tk=tk, n_chunks=n_chunks),
        scratch_shapes=s_scratch + p_scratch, name="gqa_attention", **common)
    return lax.cond(score_bound <= MAX_UNSHIFTED_SCORE, bounded, general, q, k_all, v_all, zs)


def _out_kernel(y_ref, x_ref, mod_ref, w_ref, fg_ref, o_ref):
    d = D_MODEL
    gate = mod_ref[0, :, 2 * d:3 * d]
    xo = x_ref[0] + gate * _dot(y_ref[0], w_ref[...])
    ms = jnp.mean(xo * xo, axis=-1, keepdims=True)
    o_ref[0] = xo * lax.rsqrt(ms + RMS_EPS) * fg_ref[...]


def _out_proj(y, x, mod, w, fg, *, tile):
    bsz, seq_len, d = x.shape
    e = D_INNER
    return pl.pallas_call(
        _out_kernel,
        grid=(bsz, seq_len // tile),
        in_specs=[
            pl.BlockSpec((1, tile, e), lambda b, j: (b, j, 0)),
            pl.BlockSpec((1, tile, d), lambda b, j: (b, j, 0)),
            pl.BlockSpec((1, 1, 3 * d), lambda b, j: (b, 0, 0)),
            _const_spec((e, d)),
            _const_spec((1, d)),
        ],
        out_specs=pl.BlockSpec((1, tile, d), lambda b, j: (b, j, 0)),
        out_shape=jax.ShapeDtypeStruct((bsz, seq_len, d), _F32),
        compiler_params=pltpu.CompilerParams(
            dimension_semantics=("parallel", "parallel"), vmem_limit_bytes=VMEM_LIMIT),
        name="out_proj_norm",
    )(y, x, mod, w, fg)


def _rope_tables(length):
    rows = length // GRID_W
    row = jnp.broadcast_to(jnp.arange(rows)[:, None], (rows, GRID_W)).reshape(-1)
    col = jnp.broadcast_to(jnp.arange(GRID_W)[None, :], (rows, GRID_W)).reshape(-1)
    inv_freq = ROPE_THETA ** (-jnp.arange(0, ROPE_AXIS_DIM, 2, dtype=_F32) / ROPE_AXIS_DIM)
    ang_r = row.astype(_F32)[:, None] * inv_freq
    ang_c = col.astype(_F32)[:, None] * inv_freq
    cos = jnp.concatenate([jnp.cos(ang_r)] * 2 + [jnp.cos(ang_c)] * 2, axis=-1)
    sin = jnp.concatenate([-jnp.sin(ang_r), jnp.sin(ang_r), -jnp.sin(ang_c), jnp.sin(ang_c)], axis=-1)
    return cos, sin


def kernel(x, c, ctx, c_ctx, l0_norm_g, l0_ada_w, l0_ada_b, l0_w_in, l0_b_in, l0_dw_w, l0_dw_b, l0_ln_g, l0_ln_b, l0_w_out, l0_b_out, l1_norm_g, l1_ada_w, l1_ada_b, l1_w_in, l1_q_norm_g, l1_k_norm_g, l1_w_out, final_norm_g):
    bsz, seq_len, d = x.shape
    ctx_len = ctx.shape[1]
    row = lambda a: a.reshape(1, -1)

    cond = jnp.zeros((COND_ROWS, d), _F32).at[:bsz].set(c).at[bsz].set(c_ctx)
    mod0, mod1 = _ada(cond, l0_ada_w, l0_ada_b, l1_ada_w, l1_ada_b)
    mod0 = mod0.reshape(COND_ROWS, 1, 3 * d)
    mod1 = mod1.reshape(COND_ROWS, 1, 3 * d)

    dww = jnp.broadcast_to(l0_dw_w[:, None, :], (CONV_WIDTH, SUBLANES, D_INNER))
    l0_args = (row(l0_norm_g), l0_w_in.astype(_BF16), row(l0_b_in), dww, row(l0_dw_b),
               row(l0_ln_g), row(l0_ln_b), l0_w_out.astype(_BF16), row(l0_b_out))
    x1 = _layer0(x, mod0, None, *l0_args, tile=512)
    ctx1 = _layer0(ctx, mod0, bsz, *l0_args, tile=ctx_len)

    qg = row(l1_q_norm_g) * (HEAD_DIM ** -0.5 * math.log2(math.e))
    kg = row(l1_k_norm_g)
    w1 = l1_w_in.astype(_BF16)
    cos, sin = _rope_tables(seq_len)
    q, k, v, zs = _layer1_in(x1, mod1, None, row(l1_norm_g), w1, qg, kg, cos, sin,
                             tile=512, latent=True)
    k_ctx, v_ctx = _layer1_in(ctx1, mod1, bsz, row(l1_norm_g), w1, None, kg, None, None,
                              tile=ctx_len, latent=False)

    k_all = jnp.concatenate([k_ctx, k], axis=1)
    v_all = jnp.concatenate([v_ctx, v], axis=1)
    score_bound = HEAD_DIM * jnp.max(jnp.abs(qg)) * jnp.max(jnp.abs(kg)) * SCORE_BOUND_MARGIN
    y = _attention(q, k_all, v_all, zs, score_bound, tq=256, tk=256)
    return _out_proj(y, x1, mod1, l1_w_out.astype(_BF16), row(final_norm_g), tile=512)
```

```python
import functools
import math

import jax
import jax.numpy as jnp
from jax import lax
from jax.experimental import pallas as pl
from jax.experimental.pallas import tpu as pltpu

D_MODEL = 1024
GRID_W = 64
D_INNER = 2048
CONV_WIDTH = 31
HEAD_DIM = 128
N_Q_HEADS = 16
N_KV_HEADS = 4
GROUP = 4
KV_DIM = 512
ROPE_AXIS_DIM = 64
ROPE_THETA = 10000.0
RMS_EPS = 1e-6
LN_EPS = 1e-5

SUBLANES = 8
HALO = 16
COND_ROWS = 16
VMEM_LIMIT = 56 * 1024 * 1024
MAX_UNSHIFTED_SCORE = 64.0
SCORE_BOUND_MARGIN = 1.02

_BF16 = jnp.bfloat16
_F32 = jnp.float32


def _sigmoid(x):
    return 1.0 / (1.0 + jnp.exp(-x))


def _silu(x):
    return x * _sigmoid(x)


def _dot(a, b):
    return jnp.dot(a, b, preferred_element_type=_F32)


def _const_spec(shape):
    nd = len(shape)
    return pl.BlockSpec(shape, lambda *_: (0,) * nd, pipeline_mode=pl.Buffered(1))


def _ada_kernel(cond_ref, w0_ref, b0_ref, w1_ref, b1_ref, o0_ref, o1_ref):
    s = _silu(cond_ref[...])
    o0_ref[...] = jnp.dot(s, w0_ref[...], preferred_element_type=_F32,
                          precision=lax.Precision.HIGHEST) + b0_ref[...]
    o1_ref[...] = jnp.dot(s, w1_ref[...], preferred_element_type=_F32,
                          precision=lax.Precision.HIGHEST) + b1_ref[...]


def _ada(cond, w0, b0, w1, b1):
    d = D_MODEL
    wspec = pl.BlockSpec((d, d), lambda n: (0, n))
    bspec = pl.BlockSpec((1, d), lambda n: (0, n))
    ospec = pl.BlockSpec((COND_ROWS, d), lambda n: (0, n))
    return pl.pallas_call(
        _ada_kernel,
        grid=(3,),
        in_specs=[pl.BlockSpec((COND_ROWS, d), lambda n: (0, 0)), wspec, bspec, wspec, bspec],
        out_specs=[ospec, ospec],
        out_shape=[jax.ShapeDtypeStruct((COND_ROWS, 3 * d), _F32)] * 2,
        compiler_params=pltpu.CompilerParams(vmem_limit_bytes=VMEM_LIMIT),
        name="adaln",
    )(cond, w0, b0.reshape(1, -1), w1, b1.reshape(1, -1))


def _modulate(xf, gmul, shift):
    ms = jnp.mean(xf * xf, axis=-1, keepdims=True)
    return xf * lax.rsqrt(ms + RMS_EPS) * gmul + shift


FIRST_TAP_ROW = HALO - CONV_WIDTH // 2

def _halo_tiles(tile):
    return (tile + 2 * HALO) // SUBLANES


def _l0_kernel(xm_ref, xp_ref, xn_ref, mod_ref, ng_ref, win_ref, bin_ref, dww_ref, dwb_ref,
               lng_ref, lnb_ref, wout_ref, bout_ref, o_ref,
               h_ref, vs0_ref, vs1_ref, conv_ref, *, tile, seq_len, cblk, rblk):
    d, e = D_MODEL, D_INNER
    n_cb = e // cblk
    first_tap_row = FIRST_TAP_ROW
    j = pl.program_id(1)
    shift = mod_ref[0, :, 0:d]
    scale = mod_ref[0, :, d:2 * d]
    gate = mod_ref[0, :, 2 * d:3 * d]
    gmul = ng_ref[...] * (1.0 + scale)

    xm = xm_ref[0]
    h_ref[0:HALO, :] = _modulate(xp_ref[0], gmul, shift).astype(_BF16)
    h_ref[HALO:HALO + tile, :] = _modulate(xm, gmul, shift).astype(_BF16)
    h_ref[HALO + tile:, :] = _modulate(xn_ref[0], gmul, shift).astype(_BF16)

    tok = j * tile - HALO + lax.broadcasted_iota(jnp.int32, (tile + 2 * HALO, 1), 0)
    valid = (tok >= 0) & (tok < seq_len)
    vs_bufs = (vs0_ref, vs1_ref)

    def cols(cb, base=0):
        if isinstance(cb, int):
            return pl.ds(base + cb * cblk, cblk)
        return pl.ds(base + pl.multiple_of(cb * cblk, cblk), cblk)

    def glu_stage(cb, parity):
        hh = h_ref[...]
        a = _dot(hh, win_ref[:, cols(cb)]) + bin_ref[:, cols(cb)]
        g = _dot(hh, win_ref[:, cols(cb, e)]) + bin_ref[:, cols(cb, e)]
        v = jnp.where(valid, a * _sigmoid(g), 0.0)
        n_in = v.shape[0] // SUBLANES
        v3 = v.reshape(n_in, SUBLANES, cblk)
        buf = vs_bufs[parity]
        buf[0] = v3
        for s in range(1, SUBLANES):
            rot = pltpu.roll(v3, SUBLANES - s, 1)
            buf[s, :, 0:SUBLANES - s, :] = rot[:, 0:SUBLANES - s, :]
            buf[s, 0:n_in - 1, SUBLANES - s:, :] = rot[1:, SUBLANES - s:, :]

    def conv_stage(cb, parity):
        rt = rblk // SUBLANES
        for rb in range(tile // rblk):
            acc = jnp.broadcast_to(dwb_ref[:, cols(cb)], (rt, SUBLANES, cblk))
            for k in range(CONV_WIDTH):
                r = first_tap_row + k
                t0 = rb * rt + r // SUBLANES
                acc = acc + vs_bufs[parity][r % SUBLANES, t0:t0 + rt] * dww_ref[k, :, cols(cb)]
            conv_ref[rb * rblk:(rb + 1) * rblk, cols(cb)] = acc.reshape(rblk, cblk)

    glu_stage(0, 0)

    def body(i, carry):
        glu_stage(2 * i + 1, 1)
        conv_stage(2 * i, 0)
        glu_stage(2 * i + 2, 0)
        conv_stage(2 * i + 1, 1)
        return carry

    lax.fori_loop(0, n_cb // 2 - 1, body, 0)
    glu_stage(n_cb - 1, 1)
    conv_stage(n_cb - 2, 0)
    conv_stage(n_cb - 1, 1)

    cv = conv_ref[...]
    mu = jnp.mean(cv, axis=-1, keepdims=True)
    xc = cv - mu
    rstd = lax.rsqrt(jnp.mean(xc * xc, axis=-1, keepdims=True) + LN_EPS)
    h_tile = h_ref[HALO:HALO + tile, :]
    out = jnp.broadcast_to(bout_ref[...], (tile, d))
    for cb in range(n_cb):
        z = _dot(h_tile, win_ref[:, cols(cb, 2 * e)]) + bin_ref[:, cols(cb, 2 * e)]
        y = _silu((conv_ref[:, cols(cb)] - mu) * rstd * lng_ref[:, cols(cb)] + lnb_ref[:, cols(cb)])
        out = out + _dot((y * _silu(z)).astype(_BF16), wout_ref[cb * cblk:(cb + 1) * cblk, :])
    o_ref[0] = xm + gate * out


def _layer0(x, mod, mod_row, ng, win, b_in, dww, dwb, lng, lnb, wout, bout, *, tile):
    bsz, seq_len, d = x.shape
    e = D_INNER
    nt = seq_len // tile
    hb = tile // HALO
    n_hblk = seq_len // HALO
    cblk, rblk = 256, 64
    if mod_row is None:
        mod_idx = lambda b, j: (b, 0, 0)
    else:
        mod_idx = lambda b, j: (mod_row, 0, 0)
    kern = functools.partial(_l0_kernel, tile=tile, seq_len=seq_len, cblk=cblk, rblk=rblk)
    return pl.pallas_call(
        kern,
        grid=(bsz, nt),
        in_specs=[
            pl.BlockSpec((1, tile, d), lambda b, j: (b, j, 0)),
            pl.BlockSpec((1, HALO, d), lambda b, j: (b, jnp.maximum(j * hb - 1, 0), 0)),
            pl.BlockSpec((1, HALO, d), lambda b, j: (b, jnp.minimum((j + 1) * hb, n_hblk - 1), 0)),
            pl.BlockSpec((1, 1, 3 * d), mod_idx),
            _const_spec((1, d)),
            _const_spec((d, 3 * e)),
            _const_spec((1, 3 * e)),
            _const_spec((CONV_WIDTH, SUBLANES, e)),
            _const_spec((1, e)),
            _const_spec((1, e)),
            _const_spec((1, e)),
            _const_spec((e, d)),
            _const_spec((1, d)),
        ],
        out_specs=pl.BlockSpec((1, tile, d), lambda b, j: (b, j, 0)),
        out_shape=jax.ShapeDtypeStruct((bsz, seq_len, d), _F32),
        scratch_shapes=[
            pltpu.VMEM((tile + 2 * HALO, d), _BF16),
            pltpu.VMEM((SUBLANES, _halo_tiles(tile), SUBLANES, cblk), _F32),
            pltpu.VMEM((SUBLANES, _halo_tiles(tile), SUBLANES, cblk), _F32),
            pltpu.VMEM((tile, e), _F32),
        ],
        compiler_params=pltpu.CompilerParams(
            dimension_semantics=("parallel", "parallel"), vmem_limit_bytes=VMEM_LIMIT),
        name="layer0_conv",
    )(x, x, x, mod, ng, win, b_in, dww, dwb, lng, lnb, wout, bout)


def _head_rmsnorm(xh, g):
    ms = jnp.mean(xh * xh, axis=-1, keepdims=True)
    return xh * lax.rsqrt(ms + RMS_EPS) * g


def _rope(xh, cos, sin_signed):
    lane = lax.broadcasted_iota(jnp.int32, xh.shape, 1)
    first = (lane % ROPE_AXIS_DIM) < (ROPE_AXIS_DIM // 2)
    partner = jnp.where(first, pltpu.roll(xh, HEAD_DIM - ROPE_AXIS_DIM // 2, 1),
                        pltpu.roll(xh, ROPE_AXIS_DIM // 2, 1))
    return xh * cos + partner * sin_signed


def _l1_in_kernel(*refs, tile, latent):
    d, e, hd = D_MODEL, D_INNER, HEAD_DIM
    if latent:
        (x_ref, mod_ref, ng_ref, w_ref, qg_ref, kg_ref, cos_ref, sin_ref,
         q_ref, k_ref, v_ref, zs_ref) = refs
    else:
        x_ref, mod_ref, ng_ref, w_ref, kg_ref, k_ref, v_ref = refs
    shift = mod_ref[0, :, 0:d]
    scale = mod_ref[0, :, d:2 * d]
    gmul = ng_ref[...] * (1.0 + scale)
    h = _modulate(x_ref[0], gmul, shift).astype(_BF16)

    if latent:
        cos = cos_ref[...]
        sin = sin_ref[...]
        qg = qg_ref[...]
        for c in range(e // KV_DIM):
            u = _dot(h, w_ref[:, c * KV_DIM:(c + 1) * KV_DIM])
            for i in range(KV_DIM // hd):
                qh = _rope(_head_rmsnorm(u[:, i * hd:(i + 1) * hd], qg), cos, sin)
                q_ref[0, :, c * KV_DIM + i * hd:c * KV_DIM + (i + 1) * hd] = qh.astype(_BF16)
        for c in range(e // KV_DIM):
            z = _dot(h, w_ref[:, e + 2 * KV_DIM + c * KV_DIM:e + 2 * KV_DIM + (c + 1) * KV_DIM])
            zs_ref[0, :, c * KV_DIM:(c + 1) * KV_DIM] = _silu(z).astype(_BF16)

    kg = kg_ref[...]
    ku = _dot(h, w_ref[:, e:e + KV_DIM])
    for i in range(N_KV_HEADS):
        kh = _head_rmsnorm(ku[:, i * hd:(i + 1) * hd], kg)
        if latent:
            kh = _rope(kh, cos, sin)
        k_ref[0, :, i * hd:(i + 1) * hd] = kh.astype(_BF16)
    v_ref[0] = _dot(h, w_ref[:, e + KV_DIM:e + 2 * KV_DIM]).astype(_BF16)


def _layer1_in(x, mod, mod_row, ng, w, qg, kg, cos, sin, *, tile, latent):
    bsz, seq_len, d = x.shape
    e = D_INNER
    nt = seq_len // tile
    ncol = 2 * e + 2 * KV_DIM
    if mod_row is None:
        mod_idx = lambda b, j: (b, 0, 0)
    else:
        mod_idx = lambda b, j: (mod_row, 0, 0)
    tok = lambda width: pl.BlockSpec((1, tile, width), lambda b, j: (b, j, 0))
    in_specs = [tok(d), pl.BlockSpec((1, 1, 3 * d), mod_idx), _const_spec((1, d)),
                _const_spec((d, ncol))]
    args = [x, mod, ng, w]
    kv_shape = jax.ShapeDtypeStruct((bsz, seq_len, KV_DIM), _BF16)
    if latent:
        pos = pl.BlockSpec((tile, HEAD_DIM), lambda b, j: (j, 0))
        in_specs += [_const_spec((1, HEAD_DIM)), _const_spec((1, HEAD_DIM)), pos, pos]
        args += [qg, kg, cos, sin]
        wide = jax.ShapeDtypeStruct((bsz, seq_len, e), _BF16)
        out_specs = [tok(e), tok(KV_DIM), tok(KV_DIM), tok(e)]
        out_shape = [wide, kv_shape, kv_shape, wide]
    else:
        in_specs += [_const_spec((1, HEAD_DIM))]
        args += [kg]
        out_specs = [tok(KV_DIM), tok(KV_DIM)]
        out_shape = [kv_shape, kv_shape]
    return pl.pallas_call(
        functools.partial(_l1_in_kernel, tile=tile, latent=latent),
        grid=(bsz, nt),
        in_specs=in_specs,
        out_specs=out_specs,
        out_shape=out_shape,
        compiler_params=pltpu.CompilerParams(
            dimension_semantics=("parallel", "parallel"), vmem_limit_bytes=VMEM_LIMIT),
        name="layer1_in_latent" if latent else "layer1_in_ctx",
    )(*args)


def _attn_kernel(q_ref, k_ref, v_ref, zs_ref, y_ref, s0_ref, s1_ref, p0_ref, p1_ref, *,
                 tq, tk, n_chunks):
    hd = HEAD_DIM
    q4 = q_ref[0]
    qs = jnp.concatenate([q4[:, i * hd:(i + 1) * hd] for i in range(GROUP)], axis=0)
    rows = GROUP * tq
    s_bufs = (s0_ref, s1_ref)
    p_bufs = (p0_ref, p1_ref)

    def chunk_rows(c):
        if isinstance(c, int):
            return pl.ds(c * tk, tk)
        return pl.ds(pl.multiple_of(c * tk, tk), tk)

    def tick(t, parity, carry):
        m, l, alpha, acc = carry
        static = isinstance(t, int)
        if not static or t + 2 < n_chunks:
            kc = k_ref[0, chunk_rows(t + 2), :]
            s_bufs[parity][...] = lax.dot_general(
                qs, kc, (((1,), (1,)), ((), ())), preferred_element_type=_F32)
        if not static or t >= 0:
            acc = alpha * acc + _dot(p_bufs[parity][...], v_ref[0, chunk_rows(t), :])
        if not static or 0 <= t + 1 < n_chunks:
            s = s_bufs[1 - parity][...]
            m_new = jnp.maximum(m, jnp.max(s, axis=-1, keepdims=True))
            p = jnp.exp2(s - m_new)
            alpha = jnp.exp2(m - m_new)
            l = alpha * l + jnp.sum(p, axis=-1, keepdims=True)
            p_bufs[1 - parity][...] = p.astype(_BF16)
            m = m_new
        return m, l, alpha, acc

    carry = (jnp.full((rows, 1), -jnp.inf, _F32), jnp.zeros((rows, 1), _F32),
             jnp.ones((rows, 1), _F32), jnp.zeros((rows, hd), _F32))
    carry = tick(-2, 0, carry)
    carry = tick(-1, 1, carry)

    n_pairs = (n_chunks - 2) // 2

    def body(i, carry):
        carry = tick(2 * i, 0, carry)
        return tick(2 * i + 1, 1, carry)

    carry = lax.fori_loop(0, n_pairs, body, carry)
    for t in range(2 * n_pairs, n_chunks):
        carry = tick(t, t % 2, carry)
    m, l, alpha, acc = carry
    o = acc / l
    for i in range(GROUP):
        zi = zs_ref[0, :, i * hd:(i + 1) * hd].astype(_F32)
        y_ref[0, :, i * hd:(i + 1) * hd] = (o[i * tq:(i + 1) * tq, :] * zi).astype(_BF16)


def _attn_bounded_kernel(q_ref, k_ref, v_ref, zs_ref, y_ref, p0_ref, p1_ref, *, tq, tk, n_chunks):
    hd = HEAD_DIM
    q4 = q_ref[0]
    qs = jnp.concatenate([q4[:, i * hd:(i + 1) * hd] for i in range(GROUP)], axis=0)
    rows = GROUP * tq
    p_bufs = (p0_ref, p1_ref)
    ones = jnp.ones((tk, hd), _BF16)

    def chunk_rows(c):
        if isinstance(c, int):
            return pl.ds(c * tk, tk)
        return pl.ds(pl.multiple_of(c * tk, tk), tk)

    def tick(t, parity, acc):
        static = isinstance(t, int)
        if not static or t + 1 < n_chunks:
            s = lax.dot_general(qs, k_ref[0, chunk_rows(t + 1), :], (((1,), (1,)), ((), ())),
                                preferred_element_type=_F32)
            p_bufs[1 - parity][...] = jnp.exp2(s).astype(_BF16)
        if not static or t >= 0:
            v_aug = jnp.concatenate([v_ref[0, chunk_rows(t), :], ones], axis=1)
            acc = acc + _dot(p_bufs[parity][...], v_aug)
        return acc

    acc = jnp.zeros((rows, 2 * hd), _F32)
    for t in range(-1, n_chunks):
        acc = tick(t, t % 2, acc)
    o = acc[:, :hd] / acc[:, hd:]
    for i in range(GROUP):
        zi = zs_ref[0, :, i * hd:(i + 1) * hd].astype(_F32)
        y_ref[0, :, i * hd:(i + 1) * hd] = (o[i * tq:(i + 1) * tq, :] * zi).astype(_BF16)


def _attention(q, k_all, v_all, zs, score_bound, *, tq, tk):
    bsz, seq_len, e = q.shape
    kv_len = k_all.shape[1]
    hd = HEAD_DIM
    rows = GROUP * tq
    n_chunks = kv_len // tk
    qspec = pl.BlockSpec((1, tq, KV_DIM), lambda b, g, i: (b, i, g))
    kvspec = pl.BlockSpec((1, kv_len, hd), lambda b, g, i: (b, 0, g))
    common = dict(
        grid=(bsz, N_KV_HEADS, seq_len // tq),
        in_specs=[qspec, kvspec, kvspec, qspec],
        out_specs=qspec,
        out_shape=jax.ShapeDtypeStruct((bsz, seq_len, e), _BF16),
        compiler_params=pltpu.CompilerParams(
            dimension_semantics=("parallel", "parallel", "parallel"), vmem_limit_bytes=VMEM_LIMIT),
    )
    p_scratch = [pltpu.VMEM((rows, tk), _BF16), pltpu.VMEM((rows, tk), _BF16)]
    s_scratch = [pltpu.VMEM((rows, tk), _F32), pltpu.VMEM((rows, tk), _F32)]
    bounded = pl.pallas_call(
        functools.partial(_attn_bounded_kernel, tq=tq, tk=tk, n_chunks=n_chunks),
        scratch_shapes=p_scratch, name="gqa_attention_bounded", **common)
    general = pl.pallas_call(
        functools.partial(_attn_kernel, tq=tq, tk=tk, n_chunks=n_chunks),
        scratch_shapes=s_scratch + p_scratch, name="gqa_attention", **common)
    return lax.cond(score_bound <= MAX_UNSHIFTED_SCORE, bounded, general, q, k_all, v_all, zs)


def _out_kernel(y_ref, x_ref, mod_ref, w_ref, fg_ref, o_ref):
    d = D_MODEL
    gate = mod_ref[0, :, 2 * d:3 * d]
    xo = x_ref[0] + gate * _dot(y_ref[0], w_ref[...])
    ms = jnp.mean(xo * xo, axis=-1, keepdims=True)
    o_ref[0] = xo * lax.rsqrt(ms + RMS_EPS) * fg_ref[...]


def _out_proj(y, x, mod, w, fg, *, tile):
    bsz, seq_len, d = x.shape
    e = D_INNER
    return pl.pallas_call(
        _out_kernel,
        grid=(bsz, seq_len // tile),
        in_specs=[
            pl.BlockSpec((1, tile, e), lambda b, j: (b, j, 0)),
            pl.BlockSpec((1, tile, d), lambda b, j: (b, j, 0)),
            pl.BlockSpec((1, 1, 3 * d), lambda b, j: (b, 0, 0)),
            _const_spec((e, d)),
            _const_spec((1, d)),
        ],
        out_specs=pl.BlockSpec((1, tile, d), lambda b, j: (b, j, 0)),
        out_shape=jax.ShapeDtypeStruct((bsz, seq_len, d), _F32),
        compiler_params=pltpu.CompilerParams(
            dimension_semantics=("parallel", "parallel"), vmem_limit_bytes=VMEM_LIMIT),
        name="out_proj_norm",
    )(y, x, mod, w, fg)


def _rope_tables(length):
    rows = length // GRID_W
    row = jnp.broadcast_to(jnp.arange(rows)[:, None], (rows, GRID_W)).reshape(-1)
    col = jnp.broadcast_to(jnp.arange(GRID_W)[None, :], (rows, GRID_W)).reshape(-1)
    inv_freq = ROPE_THETA ** (-jnp.arange(0, ROPE_AXIS_DIM, 2, dtype=_F32) / ROPE_AXIS_DIM)
    ang_r = row.astype(_F32)[:, None] * inv_freq
    ang_c = col.astype(_F32)[:, None] * inv_freq
    cos = jnp.concatenate([jnp.cos(ang_r)] * 2 + [jnp.cos(ang_c)] * 2, axis=-1)
    sin = jnp.concatenate([-jnp.sin(ang_r), jnp.sin(ang_r), -jnp.sin(ang_c), jnp.sin(ang_c)], axis=-1)
    return cos, sin


def kernel(x, c, ctx, c_ctx, l0_norm_g, l0_ada_w, l0_ada_b, l0_w_in, l0_b_in, l0_dw_w, l0_dw_b, l0_ln_g, l0_ln_b, l0_w_out, l0_b_out, l1_norm_g, l1_ada_w, l1_ada_b, l1_w_in, l1_q_norm_g, l1_k_norm_g, l1_w_out, final_norm_g):
    bsz, seq_len, d = x.shape
    ctx_len = ctx.shape[1]
    row = lambda a: a.reshape(1, -1)

    cond = jnp.zeros((COND_ROWS, d), _F32).at[:bsz].set(c).at[bsz].set(c_ctx)
    mod0, mod1 = _ada(cond, l0_ada_w, l0_ada_b, l1_ada_w, l1_ada_b)
    mod0 = mod0.reshape(COND_ROWS, 1, 3 * d)
    mod1 = mod1.reshape(COND_ROWS, 1, 3 * d)

    dww = jnp.broadcast_to(l0_dw_w[:, None, :], (CONV_WIDTH, SUBLANES, D_INNER))
    l0_args = (row(l0_norm_g), l0_w_in.astype(_BF16), row(l0_b_in), dww, row(l0_dw_b),
               row(l0_ln_g), row(l0_ln_b), l0_w_out.astype(_BF16), row(l0_b_out))
    x1 = _layer0(x, mod0, None, *l0_args, tile=512)
    ctx1 = _layer0(ctx, mod0, bsz, *l0_args, tile=ctx_len)

    qg = row(l1_q_norm_g) * (HEAD_DIM ** -0.5 * math.log2(math.e))
    kg = row(l1_k_norm_g)
    w1 = l1_w_in.astype(_BF16)
    cos, sin = _rope_tables(seq_len)
    q, k, v, zs = _layer1_in(x1, mod1, None, row(l1_norm_g), w1, qg, kg, cos, sin,
                             tile=512, latent=True)
    k_ctx, v_ctx = _layer1_in(ctx1, mod1, bsz, row(l1_norm_g), w1, None, kg, None, None,
                              tile=ctx_len, latent=False)

    k_all = jnp.concatenate([k_ctx, k], axis=1)
    v_all = jnp.concatenate([v_ctx, v], axis=1)
    score_bound = HEAD_DIM * jnp.max(jnp.abs(qg)) * jnp.max(jnp.abs(kg)) * SCORE_BOUND_MARGIN
    y = _attention(q, k_all, v_all, zs, score_bound, tq=512, tk=256)
    return _out_proj(y, x1, mod1, l1_w_out.astype(_BF16), row(final_norm_g), tile=512)
```
